```python
import math
import jax, jax.numpy as jnp
from jax import lax
import numpy as np

D_MODEL = 2048
BATCH = 2
SEQ = 4096
DEPTH = 1

HEAD_DIM = 128
ATTN_WIDTH = D_MODEL // 2
N_ATTN_HEADS = ATTN_WIDTH // HEAD_DIM
CONV_WIDTH = D_MODEL - ATTN_WIDTH
CONV_K = 3
IN_WIDTH = 3 * ATTN_WIDTH + 3 * CONV_WIDTH
D_FF = ((8 * D_MODEL // 3 + 255) // 256) * 256
N_MOD = 9
Q_BLOCK = 128
FFN_RES = 0.5
EPS = 1e-6

kernel_name = "hymba_stickbreak_shortconv_macaron_adaln"


def rms_norm(h):
    hf = h.astype(jnp.float32)
    hf = hf * lax.rsqrt(jnp.mean(hf * hf, axis=-1, keepdims=True) + EPS)
    return hf.astype(h.dtype)


def rms_norm_gain(h, gain):
    return rms_norm(h) * gain


def modulate(h, shift, scale):
    return h * (1.0 + scale[:, None, :]) + shift[:, None, :]


def swiglu(h, w_gu, w_down):
    gate, up = jnp.split(h @ w_gu, 2, axis=-1)
    return (jax.nn.silu(gate) * up) @ w_down


def stick_breaking_attention(q, k, v):
    seq = q.shape[2]
    inv_sqrt_d = 1.0 / math.sqrt(q.shape[-1])
    qf = q.astype(jnp.float32)
    kf = k.astype(jnp.float32)
    outs = []
    for b in range(seq // Q_BLOCK):
        t0, t1 = b * Q_BLOCK, (b + 1) * Q_BLOCK
        z = jnp.einsum('bhtd,bhsd->bhts', qf[:, :, t0:t1], kf[:, :, :t1]) * inv_sqrt_d
        t_idx = t0 + jnp.arange(Q_BLOCK)[:, None]
        s_idx = jnp.arange(t1)[None, :]
        causal = s_idx < t_idx
        log_fail = jnp.where(causal, jax.nn.log_sigmoid(-z), 0.0)
        log_tail = lax.cumsum(log_fail, axis=3, reverse=True) - log_fail
        a = jnp.where(causal, jnp.exp(jax.nn.log_sigmoid(z) + log_tail), 0.0)
        outs.append(jnp.einsum('bhts,bhsd->bhtd', a.astype(v.dtype), v[:, :, :t1]))
    return jnp.concatenate(outs, axis=2)


def causal_short_conv(u, w):
    ch = u.shape[-1]
    return lax.conv_general_dilated(
        u, w[:, None, :], window_strides=(1,), padding=[(CONV_K - 1, 0)],
        dimension_numbers=('NWC', 'WIO', 'NWC'), feature_group_count=ch)


def hybrid_mixer(h, w_in, q_norm_w, k_norm_w, conv_w, w_out):
    bsz, seq, _ = h.shape
    proj = h @ w_in
    offs = np.cumsum([ATTN_WIDTH, ATTN_WIDTH, ATTN_WIDTH, CONV_WIDTH, CONV_WIDTH])
    q, k, v, gate_b, gate_c, u = jnp.split(proj, list(offs), axis=-1)
    def heads(t):
        return t.reshape(bsz, seq, N_ATTN_HEADS, HEAD_DIM)
    q = rms_norm_gain(heads(q), q_norm_w).transpose(0, 2, 1, 3)
    k = rms_norm_gain(heads(k), k_norm_w).transpose(0, 2, 1, 3)
    v = heads(v).transpose(0, 2, 1, 3)
    attn = stick_breaking_attention(q, k, v).transpose(0, 2, 1, 3).reshape(bsz, seq, ATTN_WIDTH)
    conv = gate_b * causal_short_conv(gate_c * u, conv_w)
    return jnp.concatenate([attn, conv], axis=-1) @ w_out


def setup_inputs(seed: int = 0) -> dict:
    key = jax.random.key(seed)
    ks = jax.random.split(key, 13)
    f32 = jnp.float32
    def nrm(k, shape, scale):
        return jax.random.normal(k, shape, f32) * scale
    return {
        "x": nrm(ks[0], (BATCH, SEQ, D_MODEL), 1.0),
        "c": nrm(ks[1], (BATCH, D_MODEL), 1.0),
        "w_ada": nrm(ks[2], (DEPTH, D_MODEL, N_MOD * D_MODEL), 0.5 * D_MODEL ** -0.5),
        "b_ada": nrm(ks[3], (DEPTH, N_MOD * D_MODEL), 0.02),
        "w1_gu": nrm(ks[4], (DEPTH, D_MODEL, 2 * D_FF), D_MODEL ** -0.5),
        "w1_down": nrm(ks[5], (DEPTH, D_FF, D_MODEL), D_FF ** -0.5),
        "w_in": nrm(ks[6], (DEPTH, D_MODEL, IN_WIDTH), D_MODEL ** -0.5),
        "q_norm_w": 1.0 + nrm(ks[7], (DEPTH, HEAD_DIM), 0.02),
        "k_norm_w": 1.0 + nrm(ks[8], (DEPTH, HEAD_DIM), 0.02),
        "conv_w": nrm(ks[9], (DEPTH, CONV_K, CONV_WIDTH), CONV_K ** -0.5),
        "w_out": nrm(ks[10], (DEPTH, D_MODEL, D_MODEL), D_MODEL ** -0.5),
        "w2_gu": nrm(ks[11], (DEPTH, D_MODEL, 2 * D_FF), D_MODEL ** -0.5),
        "w2_down": nrm(ks[12], (DEPTH, D_FF, D_MODEL), D_FF ** -0.5),
    }


def reference(x, c, w_ada, b_ada, w1_gu, w1_down, w_in, q_norm_w, k_norm_w,
              conv_w, w_out, w2_gu, w2_down):
    c_act = jax.nn.silu(c)
    for l in range(DEPTH):
        mod = c_act @ w_ada[l] + b_ada[l]
        (sh1, sc1, g1, sh2, sc2, g2, sh3, sc3, g3) = jnp.split(mod, N_MOD, axis=-1)
        h = modulate(rms_norm(x), sh1, sc1)
        x = x + FFN_RES * g1[:, None, :] * swiglu(h, w1_gu[l], w1_down[l])
        h = modulate(rms_norm(x), sh2, sc2)
        x = x + g2[:, None, :] * hybrid_mixer(h, w_in[l], q_norm_w[l], k_norm_w[l], conv_w[l], w_out[l])
        h = modulate(rms_norm(x), sh3, sc3)
        x = x + FFN_RES * g3[:, None, :] * swiglu(h, w2_gu[l], w2_down[l])
    return x
```

```python
import functools
import math

import jax
import jax.numpy as jnp
from jax import lax
from jax.experimental import pallas as pl
from jax.experimental.pallas import tpu as pltpu

F32 = jnp.float32
BF16 = jnp.bfloat16

HEAD_DIM = 128
CONV_K = 3
N_MOD = 9
FFN_RES = 0.5
EPS = 1e-6

V7X_VMEM_BYTES = 64 * 1024 * 1024
V7X_VMEM_LIMIT_BYTES = 56 * 1024 * 1024
BF16_SUBLANE_TILE = 16

ADA_TN = 1024
FFN_TM = 512
FFN_TF = 512
PROJ_TM = 512
PROJ_TN = 1024
ATT_T = 256
OUT_TM = 512
ROW_CHUNK = 256


def _params(semantics, vmem_bytes):
    return pltpu.CompilerParams(dimension_semantics=semantics,
                                vmem_limit_bytes=min(int(vmem_bytes), V7X_VMEM_LIMIT_BYTES))


def _rms_mod_rows(x_ref, mod_ref, h_ref, shift_row, rows):
    xv = x_ref[rows, :]
    ms = jnp.mean(xv * xv, axis=-1, keepdims=True)
    hn = xv * lax.rsqrt(ms + EPS)
    shift = mod_ref[0, shift_row:shift_row + 1, :]
    scale = mod_ref[0, shift_row + 1:shift_row + 2, :]
    h_ref[rows, :] = (hn * (1.0 + scale) + shift).astype(BF16)


def _adaln_kernel(c_ref, w_ref, b_ref, o_ref):
    cv = c_ref[...]
    c_act = (cv * jax.nn.sigmoid(cv)).astype(BF16)
    o_ref[...] = jnp.dot(c_act, w_ref[...].astype(BF16), preferred_element_type=F32) + b_ref[...]


def _adaln(c, w_ada, b_ada):
    bsz, d = c.shape
    n = w_ada.shape[1]
    return pl.pallas_call(
        _adaln_kernel,
        grid=(n // ADA_TN,),
        in_specs=[pl.BlockSpec((bsz, d), lambda j: (0, 0)),
                  pl.BlockSpec((d, ADA_TN), lambda j: (0, j)),
                  pl.BlockSpec((1, ADA_TN), lambda j: (0, j))],
        out_specs=pl.BlockSpec((bsz, ADA_TN), lambda j: (0, j)),
        out_shape=jax.ShapeDtypeStruct((bsz, n), F32),
        compiler_params=_params(("arbitrary",), 4 * d * ADA_TN * 4),
        name="adaln",
    )(c, w_ada, b_ada.reshape(1, n))


def _ffn_kernel(x_ref, mod_ref, wg_ref, wu_ref, wd_ref, o_ref, h_ref, *, shift_row, n_f):
    f = pl.program_id(1)
    tm = x_ref.shape[0]

    @pl.when(f == 0)
    def _():
        for r in range(tm // ROW_CHUNK):
            _rms_mod_rows(x_ref, mod_ref, h_ref, shift_row, pl.ds(r * ROW_CHUNK, ROW_CHUNK))

    h = h_ref[...]
    g = jnp.dot(h, wg_ref[...], preferred_element_type=F32)
    u = jnp.dot(h, wu_ref[...], preferred_element_type=F32)
    act = (g * jax.nn.sigmoid(g) * u).astype(BF16)
    part = jnp.dot(act, wd_ref[...], preferred_element_type=F32)

    @pl.when(f == 0)
    def _():
        o_ref[...] = part

    @pl.when(f > 0)
    def _():
        o_ref[...] += part

    @pl.when(f == n_f - 1)
    def _():
        gate = FFN_RES * mod_ref[0, shift_row + 2:shift_row + 3, :]
        o_ref[...] = x_ref[...] + gate * o_ref[...]


def _ffn(x2d, mod, w_gu, w_down, *, shift_row, seq):
    t, d = x2d.shape
    d_ff = w_down.shape[0]
    n_f = d_ff // FFN_TF
    tm = FFN_TM
    vmem = (2 * tm * d * 4) * 2 + tm * d * 2 + 2 * 3 * d * FFN_TF * 2 + 6 * tm * FFN_TF * 4 + (4 << 20)
    return pl.pallas_call(
        functools.partial(_ffn_kernel, shift_row=shift_row, n_f=n_f),
        grid=(t // tm, n_f),
        in_specs=[pl.BlockSpec((tm, d), lambda i, f: (i, 0)),
                  pl.BlockSpec((1, N_MOD, d), lambda i, f: (i * tm // seq, 0, 0)),
                  pl.BlockSpec((d, FFN_TF), lambda i, f: (0, f)),
                  pl.BlockSpec((d, FFN_TF), lambda i, f: (0, f + n_f)),
                  pl.BlockSpec((FFN_TF, d), lambda i, f: (f, 0))],
        out_specs=pl.BlockSpec((tm, d), lambda i, f: (i, 0)),
        out_shape=jax.ShapeDtypeStruct((t, d), F32),
        scratch_shapes=[pltpu.VMEM((tm, d), BF16)],
        compiler_params=_params(("parallel", "arbitrary"), vmem),
        name="ffn",
    )(x2d, mod, w_gu, w_gu, w_down)


def _head_rms_gain(p, gain, out_scale):
    outs = []
    for hd in range(p.shape[1] // HEAD_DIM):
        ph = p[:, hd * HEAD_DIM:(hd + 1) * HEAD_DIM]
        ms = jnp.mean(ph * ph, axis=-1, keepdims=True)
        outs.append(ph * lax.rsqrt(ms + EPS) * (gain * out_scale))
    return jnp.concatenate(outs, axis=1)


def _in_proj_kernel(x_ref, mod_ref, w_ref, qg_ref, kg_ref, o_ref, h_ref, *, shift_row, q_scale):
    n = pl.program_id(1)
    tm = x_ref.shape[0]

    @pl.when(n == 0)
    def _():
        for r in range(tm // ROW_CHUNK):
            _rms_mod_rows(x_ref, mod_ref, h_ref, shift_row, pl.ds(r * ROW_CHUNK, ROW_CHUNK))

    p = jnp.dot(h_ref[...], w_ref[...], preferred_element_type=F32)

    @pl.when(n == 0)
    def _():
        o_ref[...] = _head_rms_gain(p, qg_ref[...], q_scale).astype(BF16)

    @pl.when(n == 1)
    def _():
        o_ref[...] = _head_rms_gain(p, kg_ref[...], 1.0).astype(BF16)

    @pl.when(n > 1)
    def _():
        o_ref[...] = p.astype(BF16)


def _in_proj(x2d, mod, w_in, q_gain, k_gain, *, shift_row, seq):
    t, d = x2d.shape
    n_cols = w_in.shape[1]
    tm = PROJ_TM
    vmem = 2 * tm * d * 4 + tm * d * 2 + 2 * d * PROJ_TN * 2 + 2 * tm * PROJ_TN * 2 + 6 * tm * PROJ_TN * 4 + (4 << 20)
    return pl.pallas_call(
        functools.partial(_in_proj_kernel, shift_row=shift_row, q_scale=1.0 / math.sqrt(HEAD_DIM)),
        grid=(t // tm, n_cols // PROJ_TN),
        in_specs=[pl.BlockSpec((tm, d), lambda i, n: (i, 0)),
                  pl.BlockSpec((1, N_MOD, d), lambda i, n: (i * tm // seq, 0, 0)),
                  pl.BlockSpec((d, PROJ_TN), lambda i, n: (0, n)),
                  pl.BlockSpec((1, HEAD_DIM), lambda i, n: (0, 0)),
                  pl.BlockSpec((1, HEAD_DIM), lambda i, n: (0, 0))],
        out_specs=pl.BlockSpec((tm, PROJ_TN), lambda i, n: (i, n)),
        out_shape=jax.ShapeDtypeStruct((t, n_cols), BF16),
        scratch_shapes=[pltpu.VMEM((tm, d), BF16)],
        compiler_params=_params(("parallel", "arbitrary"), vmem),
        name="in_proj",
    )(x2d, mod, w_in, q_gain, k_gain)


def _attn_tile(q, k_blk, v_blk, tri2, carry, causal_mask):
    z = lax.dot_general(q, k_blk, (((1,), (1,)), ((), ())), preferred_element_type=F32)
    lf = jnp.minimum(-z, 0.0) - jnp.log(1.0 + jnp.exp(-jnp.abs(z)))
    if causal_mask is not None:
        lf = jnp.where(causal_mask, lf, 0.0)
    hi = lf.astype(BF16)
    lo = (lf - hi.astype(F32)).astype(BF16)
    incl = jnp.dot(jnp.concatenate([hi, lo], axis=1), tri2, preferred_element_type=F32)
    a = jnp.exp(z + incl + carry)
    if causal_mask is not None:
        a = jnp.where(causal_mask, a, 0.0)
    pv = jnp.dot(a.astype(BF16), v_blk, preferred_element_type=F32)
    return pv, jnp.sum(lf, axis=-1, keepdims=True)


def _attention_kernel(q_ref, k_ref, v_ref, o_ref):
    i = pl.program_id(2)
    t = q_ref.shape[0]
    q = q_ref[...]
    row = lax.broadcasted_iota(jnp.int32, (t, t), 0)
    col = lax.broadcasted_iota(jnp.int32, (t, t), 1)
    tri = (row >= col).astype(BF16)
    tri2 = jnp.concatenate([tri, tri], axis=0)

    diag = pl.ds(pl.multiple_of(i * t, t), t)
    acc, carry = _attn_tile(q, k_ref[diag, :], v_ref[diag, :], tri2,
                            jnp.zeros((t, 1), F32), col < row)

    def body(jj, state):
        acc, carry = state
        blk = pl.ds(pl.multiple_of((i - 1 - jj) * t, t), t)
        pv, rs = _attn_tile(q, k_ref[blk, :], v_ref[blk, :], tri2, carry, None)
        return acc + pv, carry + rs

    acc, _ = lax.fori_loop(0, i, body, (acc, carry))
    o_ref[...] = acc.astype(o_ref.dtype)


def _attention(proj, *, bsz, seq, n_heads):
    t = proj.shape[0]
    nq = seq // ATT_T
    kv_blocks = seq // seq
    del kv_blocks
    width = n_heads * HEAD_DIM
    k_col0 = width // HEAD_DIM
    v_col0 = 2 * width // HEAD_DIM
    vmem = 2 * 2 * seq * HEAD_DIM * 2 + 24 * ATT_T * ATT_T * 4 + (4 << 20)
    return pl.pallas_call(
        _attention_kernel,
        grid=(bsz, n_heads, nq),
        in_specs=[pl.BlockSpec((ATT_T, HEAD_DIM), lambda b, h, i: (b * nq + i, h)),
                  pl.BlockSpec((seq, HEAD_DIM), lambda b, h, i: (b, k_col0 + h)),
                  pl.BlockSpec((seq, HEAD_DIM), lambda b, h, i: (b, v_col0 + h))],
        out_specs=pl.BlockSpec((ATT_T, HEAD_DIM), lambda b, h, i: (b * nq + i, h)),
        out_shape=jax.ShapeDtypeStruct((t, width), BF16),
        compiler_params=_params(("parallel", "parallel", "arbitrary"), vmem),
        name="attention",
    )(proj, proj, proj)


def _out_proj_kernel(x_ref, mod_ref, attn_ref, gb_ref, gc_ref, u_ref, gch_ref, uh_ref, cw_ref, w_ref,
                     o_ref, cu_ref, *, gate_row, seq):
    i = pl.program_id(0)
    tm = x_ref.shape[0]
    halo = gch_ref.shape[0]
    width = attn_ref.shape[1]

    prev = gch_ref[...].astype(F32) * uh_ref[...].astype(F32)
    seq_start = (i * tm) % seq == 0
    cu_ref[0:halo, :] = jnp.where(seq_start, 0.0, prev)
    cu_ref[halo:halo + tm, :] = gc_ref[...].astype(F32) * u_ref[...].astype(F32)

    conv = cu_ref[halo:halo + tm, :] * cw_ref[CONV_K - 1:CONV_K, :]
    for k in range(CONV_K - 1):
        back = CONV_K - 1 - k
        conv = conv + cu_ref[halo - back:halo - back + tm, :] * cw_ref[k:k + 1, :]
    y = (gb_ref[...].astype(F32) * conv).astype(BF16)

    mix = jnp.dot(attn_ref[...], w_ref[0:width, :], preferred_element_type=F32)
    mix = mix + jnp.dot(y, w_ref[width:, :], preferred_element_type=F32)
    gate = mod_ref[0, gate_row:gate_row + 1, :]
    o_ref[...] = x_ref[...] + gate * mix


def _out_proj(x2d, mod, attn, proj, conv_w, w_out, *, gate_row, seq):
    t, d = x2d.shape
    width = attn.shape[1]
    tm = OUT_TM
    halo = BF16_SUBLANE_TILE
    per = tm // halo
    b_col, c_col, u_col = 3, 4, 5

    def halo_map(col):
        return lambda i: (jnp.maximum(i * per - 1, 0), col)

    vmem = (2 * 2 * tm * d * 4 + 2 * 4 * tm * width * 2 + 2 * d * d * 2 + (tm + halo) * width * 4
            + 8 * tm * width * 4 + (4 << 20))
    return pl.pallas_call(
        functools.partial(_out_proj_kernel, gate_row=gate_row, seq=seq),
        grid=(t // tm,),
        in_specs=[pl.BlockSpec((tm, d), lambda i: (i, 0)),
                  pl.BlockSpec((1, N_MOD, d), lambda i: (i * tm // seq, 0, 0)),
                  pl.BlockSpec((tm, width), lambda i: (i, 0)),
                  pl.BlockSpec((tm, width), lambda i: (i, b_col)),
                  pl.BlockSpec((tm, width), lambda i: (i, c_col)),
                  pl.BlockSpec((tm, width), lambda i: (i, u_col)),
                  pl.BlockSpec((halo, width), halo_map(c_col)),
                  pl.BlockSpec((halo, width), halo_map(u_col)),
                  pl.BlockSpec((CONV_K, width), lambda i: (0, 0)),
                  pl.BlockSpec((d, d), lambda i: (0, 0))],
        out_specs=pl.BlockSpec((tm, d), lambda i: (i, 0)),
        out_shape=jax.ShapeDtypeStruct((t, d), F32),
        scratch_shapes=[pltpu.VMEM((tm + halo, width), F32)],
        compiler_params=_params(("parallel",), vmem),
        name="out_proj",
    )(x2d, mod, attn, proj, proj, proj, proj, proj, conv_w, w_out)


def kernel(x, c, w_ada, b_ada, w1_gu, w1_down, w_in, q_norm_w, k_norm_w, conv_w, w_out, w2_gu, w2_down):
    bsz, seq, d = x.shape
    depth = w_ada.shape[0]
    width = conv_w.shape[2]
    n_heads = (w_in.shape[2] - 3 * width) // 3 // HEAD_DIM
    assert n_heads * HEAD_DIM == width and w_in.shape[2] == 6 * width
    assert seq % max(FFN_TM, PROJ_TM, OUT_TM, ATT_T) == 0 and d % 128 == 0

    x2d = x.reshape(bsz * seq, d)
    for l in range(depth):
        mod = _adaln(c, w_ada[l], b_ada[l]).reshape(bsz, N_MOD, d)
        x2d = _ffn(x2d, mod, w1_gu[l].astype(BF16), w1_down[l].astype(BF16), shift_row=0, seq=seq)
        proj = _in_proj(x2d, mod, w_in[l].astype(BF16), q_norm_w[l].reshape(1, HEAD_DIM),
                        k_norm_w[l].reshape(1, HEAD_DIM), shift_row=3, seq=seq)
        attn = _attention(proj, bsz=bsz, seq=seq, n_heads=n_heads)
        x2d = _out_proj(x2d, mod, attn, proj, conv_w[l], w_out[l].astype(BF16), gate_row=5, seq=seq)
        x2d = _ffn(x2d, mod, w2_gu[l].astype(BF16), w2_down[l].astype(BF16), shift_row=6, seq=seq)
    return x2d.reshape(bsz, seq, d)
```

```python
import functools
import math

import jax
import jax.numpy as jnp
from jax import lax
from jax.experimental import pallas as pl
from jax.experimental.pallas import tpu as pltpu

F32 = jnp.float32
BF16 = jnp.bfloat16

HEAD_DIM = 128
CONV_K = 3
N_MOD = 9
FFN_RES = 0.5
EPS = 1e-6
LOG2E = 1.4426950408889634

F32_EXP2_UNDERFLOW = -150.0

V7X_VMEM_LIMIT_BYTES = 56 * 1024 * 1024
BF16_SUBLANE_TILE = 16

ADA_TN = 1024
FFN_TM = 512
FFN_TF = 512
PROJ_TM = 512
PROJ_TN = 1024
ATT_T = 256
ATT_HP = 8
OUT_TM = 512
ROW_CHUNK = 256
CAST_ROWS = 128
CAST_COLS = 1024


def _params(semantics, vmem_bytes):
    return pltpu.CompilerParams(dimension_semantics=semantics,
                                vmem_limit_bytes=min(int(vmem_bytes), V7X_VMEM_LIMIT_BYTES))


def _rms_mod_rows(x_ref, mod_ref, h_ref, shift_row, rows):
    xv = x_ref[rows, :]
    ms = jnp.mean(xv * xv, axis=-1, keepdims=True)
    hn = xv * lax.rsqrt(ms + EPS)
    shift = mod_ref[0, shift_row:shift_row + 1, :]
    scale = mod_ref[0, shift_row + 1:shift_row + 2, :]
    h_ref[rows, :] = (hn * (1.0 + scale) + shift).astype(BF16)


def _adaln_kernel(c_ref, w_ref, b_ref, o_ref):
    cv = c_ref[...]
    c_act = (cv * jax.nn.sigmoid(cv)).astype(BF16)
    o_ref[...] = jnp.dot(c_act, w_ref[...].astype(BF16), preferred_element_type=F32) + b_ref[...]


def _adaln(c, w_ada, b_ada):
    bsz, d = c.shape
    n = w_ada.shape[1]
    return pl.pallas_call(
        _adaln_kernel,
        grid=(n // ADA_TN,),
        in_specs=[pl.BlockSpec((bsz, d), lambda j: (0, 0)),
                  pl.BlockSpec((d, ADA_TN), lambda j: (0, j)),
                  pl.BlockSpec((1, ADA_TN), lambda j: (0, j))],
        out_specs=pl.BlockSpec((bsz, ADA_TN), lambda j: (0, j)),
        out_shape=jax.ShapeDtypeStruct((bsz, n), F32),
        compiler_params=_params(("arbitrary",), 4 * d * ADA_TN * 4),
        name="adaln",
    )(c, w_ada, b_ada.reshape(1, n))


def _ffn_kernel(x_ref, mod_ref, wg_ref, wu_ref, wd_ref, *rest, shift_row, n_f, cast_chunks):
    n_cast = len(cast_chunks)
    cast_in = rest[:n_cast]
    o_ref = rest[n_cast]
    cast_out = rest[n_cast + 1:2 * n_cast + 1]
    h_ref = rest[2 * n_cast + 1]
    i = pl.program_id(0)
    f = pl.program_id(1)
    tm = x_ref.shape[0]

    @pl.when(f == 0)
    def _():
        for r in range(tm // ROW_CHUNK):
            _rms_mod_rows(x_ref, mod_ref, h_ref, shift_row, pl.ds(r * ROW_CHUNK, ROW_CHUNK))
        o_ref[...] = jnp.zeros_like(o_ref)

    h = h_ref[...]
    g = jnp.dot(h, wg_ref[...], preferred_element_type=F32)
    u = jnp.dot(h, wu_ref[...], preferred_element_type=F32)
    act = (g * jax.nn.sigmoid(g) * u).astype(BF16)
    o_ref[...] += jnp.dot(act, wd_ref[...], preferred_element_type=F32)

    for src_ref, dst_ref in zip(cast_in, cast_out):
        dst_ref[...] = src_ref[...].astype(BF16)

    @pl.when(f == n_f - 1)
    def _():
        gate = FFN_RES * mod_ref[0, shift_row + 2:shift_row + 3, :]
        o_ref[...] = x_ref[...] + gate * o_ref[...]


def _ffn(x2d, mod, w_gu, w_down, *, shift_row, seq, cast=()):
    t, d = x2d.shape
    d_ff = w_down.shape[0]
    n_f = d_ff // FFN_TF
    tm = FFN_TM
    n_steps = (t // tm) * n_f
    chunk_elems = CAST_ROWS * CAST_COLS
    cast_chunks = tuple(w.size // chunk_elems for w in cast)
    assert all(w.size % chunk_elems == 0 and n <= n_steps for w, n in zip(cast, cast_chunks))

    def cast_spec(n_chunks):
        return pl.BlockSpec((CAST_ROWS, CAST_COLS), lambda i, f: (jnp.minimum(i * n_f + f, n_chunks - 1), 0))

    vmem = ((2 * tm * d * 4) * 2 + tm * d * 2 + 2 * 3 * d * FFN_TF * 2 + 6 * tm * FFN_TF * 4
            + len(cast) * 2 * chunk_elems * 6 + (4 << 20))
    outs = pl.pallas_call(
        functools.partial(_ffn_kernel, shift_row=shift_row, n_f=n_f, cast_chunks=cast_chunks),
        grid=(t // tm, n_f),
        in_specs=[pl.BlockSpec((tm, d), lambda i, f: (i, 0)),
                  pl.BlockSpec((1, N_MOD, d), lambda i, f: (i * tm // seq, 0, 0)),
                  pl.BlockSpec((d, FFN_TF), lambda i, f: (0, f)),
                  pl.BlockSpec((d, FFN_TF), lambda i, f: (0, f + n_f)),
                  pl.BlockSpec((FFN_TF, d), lambda i, f: (f, 0))]
                 + [cast_spec(n) for n in cast_chunks],
        out_specs=[pl.BlockSpec((tm, d), lambda i, f: (i, 0))] + [cast_spec(n) for n in cast_chunks],
        out_shape=[jax.ShapeDtypeStruct((t, d), F32)]
                  + [jax.ShapeDtypeStruct((n * CAST_ROWS, CAST_COLS), BF16) for n in cast_chunks],
        scratch_shapes=[pltpu.VMEM((tm, d), BF16)],
        compiler_params=_params(("arbitrary", "arbitrary"), vmem),
        name="ffn",
    )(x2d, mod, w_gu, w_gu, w_down, *[w.reshape(-1, CAST_COLS) for w in cast])
    return outs[0], [o.reshape(w.shape) for o, w in zip(outs[1:], cast)]


def _head_rms_gain(p, gain):
    outs = []
    for hd in range(p.shape[1] // HEAD_DIM):
        ph = p[:, hd * HEAD_DIM:(hd + 1) * HEAD_DIM]
        ms = jnp.mean(ph * ph, axis=-1, keepdims=True)
        outs.append(ph * lax.rsqrt(ms + EPS) * gain)
    return jnp.concatenate(outs, axis=1)


def _in_proj_kernel(x_ref, mod_ref, w_ref, qg_ref, kg_ref, o_ref, h_ref, *, shift_row, q_scale):
    n = pl.program_id(1)
    tm = x_ref.shape[0]

    @pl.when(n == 0)
    def _():
        for r in range(tm // ROW_CHUNK):
            _rms_mod_rows(x_ref, mod_ref, h_ref, shift_row, pl.ds(r * ROW_CHUNK, ROW_CHUNK))

    @pl.when(n < 2)
    def _():
        gain = jnp.where(n == 0, qg_ref[...] * q_scale, kg_ref[...])
        p = jnp.dot(h_ref[...], w_ref[...], preferred_element_type=F32)
        o_ref[...] = _head_rms_gain(p, gain).astype(BF16)

    @pl.when(n >= 2)
    def _():
        o_ref[...] = jnp.dot(h_ref[...], w_ref[...], preferred_element_type=F32).astype(BF16)


def _in_proj(x2d, mod, w_in, q_gain, k_gain, *, shift_row, seq):
    t, d = x2d.shape
    n_cols = w_in.shape[1]
    tm = PROJ_TM
    vmem = 2 * tm * d * 4 + tm * d * 2 + 2 * d * PROJ_TN * 2 + 2 * tm * PROJ_TN * 2 + 6 * tm * PROJ_TN * 4 + (4 << 20)
    q_scale = LOG2E / math.sqrt(HEAD_DIM)
    return pl.pallas_call(
        functools.partial(_in_proj_kernel, shift_row=shift_row, q_scale=q_scale),
        grid=(t // tm, n_cols // PROJ_TN),
        in_specs=[pl.BlockSpec((tm, d), lambda i, n: (i, 0)),
                  pl.BlockSpec((1, N_MOD, d), lambda i, n: (i * tm // seq, 0, 0)),
                  pl.BlockSpec((d, PROJ_TN), lambda i, n: (0, n)),
                  pl.BlockSpec((1, HEAD_DIM), lambda i, n: (0, 0)),
                  pl.BlockSpec((1, HEAD_DIM), lambda i, n: (0, 0))],
        out_specs=pl.BlockSpec((tm, PROJ_TN), lambda i, n: (i, n)),
        out_shape=jax.ShapeDtypeStruct((t, n_cols), BF16),
        scratch_shapes=[pltpu.VMEM((tm, d), BF16)],
        compiler_params=_params(("parallel", "arbitrary"), vmem),
        name="in_proj",
    )(x2d, mod, w_in, q_gain, k_gain)


def _attn_step(qs, ks, vs, neg_tri2, carries, causal_mask):
    n_h = len(qs)
    z, hilo, rs, neg_incl, a, pv = ({} for _ in range(6))

    def scores(h):
        z[h] = lax.dot_general(qs[h], ks[h], (((1,), (1,)), ((), ())), preferred_element_type=F32)

    def softplus(h):
        sp = jnp.maximum(z[h], 0.0) + jnp.log(1.0 + jnp.exp2(jnp.minimum(z[h], -z[h]))) * LOG2E
        if causal_mask is not None:
            sp = jnp.where(causal_mask, sp, 0.0)
        hi = sp.astype(BF16)
        lo = (sp - hi.astype(F32)).astype(BF16)
        hilo[h] = jnp.concatenate([hi, lo], axis=1)
        rs[h] = jnp.sum(sp, axis=-1, keepdims=True)

    def suffix_sum(h):
        neg_incl[h] = jnp.dot(hilo[h], neg_tri2, preferred_element_type=F32)

    def weights(h):
        w = jnp.exp2(z[h] + neg_incl[h] + carries[h])
        if causal_mask is not None:
            w = jnp.where(causal_mask, w, 0.0)
        a[h] = w.astype(BF16)

    def values(h):
        pv[h] = jnp.dot(a[h], vs[h], preferred_element_type=F32)

    stages = (scores, softplus, suffix_sum, weights, values)
    for wave in range(n_h + len(stages) - 1):
        for s in reversed(range(len(stages))):
            h = wave - s
            if 0 <= h < n_h:
                stages[s](h)
    return [pv[h] for h in range(n_h)], [rs[h] for h in range(n_h)]


def _attention_kernel(q_ref, k_ref, v_ref, o_ref):
    i = pl.program_id(2)
    t = q_ref.shape[0]
    n_h = q_ref.shape[1] // HEAD_DIM
    row = lax.broadcasted_iota(jnp.int32, (t, t), 0)
    col = lax.broadcasted_iota(jnp.int32, (t, t), 1)
    neg_tri = jnp.where(row >= col, -1.0, 0.0).astype(BF16)
    neg_tri2 = jnp.concatenate([neg_tri, neg_tri], axis=0)
    causal = col < row

    def lanes(hd):
        return slice(hd * HEAD_DIM, (hd + 1) * HEAD_DIM)

    qs = [q_ref[:, lanes(hd)] for hd in range(n_h)]

    def step(blk, accs, carries, mask):
        ks = [k_ref[blk, lanes(hd)] for hd in range(n_h)]
        vs = [v_ref[blk, lanes(hd)] for hd in range(n_h)]
        pvs, rss = _attn_step(qs, ks, vs, neg_tri2, carries, mask)
        accs = tuple(pvs) if accs is None else tuple(acc + pv for acc, pv in zip(accs, pvs))
        carries = tuple(cr - rs for cr, rs in zip(carries, rss))
        live = carries[0]
        for cr in carries[1:]:
            live = jnp.maximum(live, cr)
        return accs, carries, jnp.max(live)

    zero = jnp.zeros((t, 1), F32)
    accs, carries, live = step(pl.ds(pl.multiple_of(i * t, t), t), None, (zero,) * n_h, causal)

    def cond(state):
        jj, live = state[0], state[1]
        return jnp.logical_and(jj < i, live > F32_EXP2_UNDERFLOW)

    def body(state):
        jj, _, accs, carries = state
        blk = pl.ds(pl.multiple_of((i - 1 - jj) * t, t), t)
        accs, carries, live = step(blk, accs, carries, None)
        return jj + 1, live, accs, carries

    _, _, accs, _ = lax.while_loop(cond, body, (jnp.int32(0), live, accs, carries))
    for hd in range(n_h):
        o_ref[:, lanes(hd)] = accs[hd].astype(o_ref.dtype)


def _attention(proj, *, bsz, seq, n_heads):
    t = proj.shape[0]
    nq = seq // ATT_T
    group = ATT_HP * HEAD_DIM
    n_groups = n_heads // ATT_HP
    vmem = 2 * 2 * seq * group * 2 + ATT_HP * 16 * ATT_T * ATT_T * 4 + (4 << 20)
    return pl.pallas_call(
        _attention_kernel,
        grid=(bsz, n_groups, nq),
        in_specs=[pl.BlockSpec((ATT_T, group), lambda b, g, i: (b * nq + i, g)),
                  pl.BlockSpec((seq, group), lambda b, g, i: (b, n_groups + g)),
                  pl.BlockSpec((seq, group), lambda b, g, i: (b, 2 * n_groups + g))],
        out_specs=pl.BlockSpec((ATT_T, group), lambda b, g, i: (b * nq + i, g)),
        out_shape=jax.ShapeDtypeStruct((t, n_heads * HEAD_DIM), BF16),
        compiler_params=_params(("parallel", "parallel", "arbitrary"), vmem),
        name="attention",
    )(proj, proj, proj)


def _out_proj_kernel(x_ref, mod_ref, attn_ref, gb_ref, gc_ref, u_ref, gch_ref, uh_ref, cw_ref, w_ref,
                     o_ref, cu_ref, *, gate_row, seq):
    i = pl.program_id(0)
    tm = x_ref.shape[0]
    halo = gch_ref.shape[0]
    width = attn_ref.shape[1]

    prev = gch_ref[...].astype(F32) * uh_ref[...].astype(F32)
    seq_start = (i * tm) % seq == 0
    cu_ref[0:halo, :] = jnp.where(seq_start, 0.0, prev)
    cu_ref[halo:halo + tm, :] = gc_ref[...].astype(F32) * u_ref[...].astype(F32)

    conv = cu_ref[halo:halo + tm, :] * cw_ref[CONV_K - 1:CONV_K, :]
    for k in range(CONV_K - 1):
        back = CONV_K - 1 - k
        conv = conv + cu_ref[halo - back:halo - back + tm, :] * cw_ref[k:k + 1, :]
    y = (gb_ref[...].astype(F32) * conv).astype(BF16)

    mix = jnp.dot(attn_ref[...], w_ref[0:width, :], preferred_element_type=F32)
    mix = mix + jnp.dot(y, w_ref[width:, :], preferred_element_type=F32)
    gate = mod_ref[0, gate_row:gate_row + 1, :]
    o_ref[...] = x_ref[...] + gate * mix


def _out_proj(x2d, mod, attn, proj, conv_w, w_out, *, gate_row, seq):
    t, d = x2d.shape
    width = attn.shape[1]
    tm = OUT_TM
    halo = BF16_SUBLANE_TILE
    per = tm // halo
    b_col, c_col, u_col = 3, 4, 5

    def halo_map(col):
        return lambda i: (jnp.maximum(i * per - 1, 0), col)

    vmem = (2 * 2 * tm * d * 4 + 2 * 4 * tm * width * 2 + 2 * d * d * 2 + (tm + halo) * width * 4
            + 8 * tm * width * 4 + (4 << 20))
    return pl.pallas_call(
        functools.partial(_out_proj_kernel, gate_row=gate_row, seq=seq),
        grid=(t // tm,),
        in_specs=[pl.BlockSpec((tm, d), lambda i: (i, 0)),
                  pl.BlockSpec((1, N_MOD, d), lambda i: (i * tm // seq, 0, 0)),
                  pl.BlockSpec((tm, width), lambda i: (i, 0)),
                  pl.BlockSpec((tm, width), lambda i: (i, b_col)),
                  pl.BlockSpec((tm, width), lambda i: (i, c_col)),
                  pl.BlockSpec((tm, width), lambda i: (i, u_col)),
                  pl.BlockSpec((halo, width), halo_map(c_col)),
                  pl.BlockSpec((halo, width), halo_map(u_col)),
                  pl.BlockSpec((CONV_K, width), lambda i: (0, 0)),
                  pl.BlockSpec((d, d), lambda i: (0, 0))],
        out_specs=pl.BlockSpec((tm, d), lambda i: (i, 0)),
        out_shape=jax.ShapeDtypeStruct((t, d), F32),
        scratch_shapes=[pltpu.VMEM((tm + halo, width), F32)],
        compiler_params=_params(("parallel",), vmem),
        name="out_proj",
    )(x2d, mod, attn, proj, proj, proj, proj, proj, conv_w, w_out)


def kernel(x, c, w_ada, b_ada, w1_gu, w1_down, w_in, q_norm_w, k_norm_w, conv_w, w_out, w2_gu, w2_down):
    bsz, seq, d = x.shape
    depth = w_ada.shape[0]
    width = conv_w.shape[2]
    n_heads = width // HEAD_DIM
    assert w_in.shape[2] == 6 * width and n_heads % ATT_HP == 0 and width == PROJ_TN
    assert seq % max(FFN_TM, PROJ_TM, OUT_TM, ATT_T) == 0 and d % 128 == 0

    x2d = x.reshape(bsz * seq, d)
    for l in range(depth):
        mod = _adaln(c, w_ada[l], b_ada[l]).reshape(bsz, N_MOD, d)
        x2d, (w_in_b, w_out_b, w2_gu_b, w2_down_b) = _ffn(
            x2d, mod, w1_gu[l].astype(BF16), w1_down[l].astype(BF16), shift_row=0, seq=seq,
            cast=(w_in[l], w_out[l], w2_gu[l], w2_down[l]))
        proj = _in_proj(x2d, mod, w_in_b, q_norm_w[l].reshape(1, HEAD_DIM),
                        k_norm_w[l].reshape(1, HEAD_DIM), shift_row=3, seq=seq)
        attn = _attention(proj, bsz=bsz, seq=seq, n_heads=n_heads)
        x2d = _out_proj(x2d, mod, attn, proj, conv_w[l], w_out_b, gate_row=5, seq=seq)
        x2d, _ = _ffn(x2d, mod, w2_gu_b, w2_down_b, shift_row=6, seq=seq)
    return x2d.reshape(bsz, seq, d)
```

```python
import functools
import math

import jax
import jax.numpy as jnp
from jax import lax
from jax.experimental import pallas as pl
from jax.experimental.pallas import tpu as pltpu

F32 = jnp.float32
BF16 = jnp.bfloat16

HEAD_DIM = 128
CONV_K = 3
N_MOD = 9
FFN_RES = 0.5
EPS = 1e-6
LOG2E = 1.4426950408889634

F32_EXP2_UNDERFLOW = -150.0

V7X_VMEM_LIMIT_BYTES = 56 * 1024 * 1024
BF16_SUBLANE_TILE = 16

ADA_TN = 1024
FFN_TM = 512
FFN_TF = 512
PROJ_TM = 1024
PROJ_TN = 1024
ATT_T = 256
ATT_HP = 8
OUT_TM = 512
ROW_CHUNK = 256


def _params(semantics, vmem_bytes):
    return pltpu.CompilerParams(dimension_semantics=semantics,
                                vmem_limit_bytes=min(int(vmem_bytes), V7X_VMEM_LIMIT_BYTES))


def _rms_mod_rows(x_ref, mod_ref, h_ref, shift_row, rows):
    xv = x_ref[rows, :]
    ms = jnp.mean(xv * xv, axis=-1, keepdims=True)
    hn = xv * lax.rsqrt(ms + EPS)
    shift = mod_ref[0, shift_row:shift_row + 1, :]
    scale = mod_ref[0, shift_row + 1:shift_row + 2, :]
    h_ref[rows, :] = (hn * (1.0 + scale) + shift).astype(BF16)


def _adaln_kernel(c_ref, w_ref, b_ref, o_ref):
    cv = c_ref[...]
    c_act = (cv * jax.nn.sigmoid(cv)).astype(BF16)
    o_ref[...] = jnp.dot(c_act, w_ref[...].astype(BF16), preferred_element_type=F32) + b_ref[...]


def _adaln(c, w_ada, b_ada):
    bsz, d = c.shape
    n = w_ada.shape[1]
    return pl.pallas_call(
        _adaln_kernel,
        grid=(n // ADA_TN,),
        in_specs=[pl.BlockSpec((bsz, d), lambda j: (0, 0)),
                  pl.BlockSpec((d, ADA_TN), lambda j: (0, j)),
                  pl.BlockSpec((1, ADA_TN), lambda j: (0, j))],
        out_specs=pl.BlockSpec((bsz, ADA_TN), lambda j: (0, j)),
        out_shape=jax.ShapeDtypeStruct((bsz, n), F32),
        compiler_params=_params(("arbitrary",), 4 * d * ADA_TN * 4),
        name="adaln",
    )(c, w_ada, b_ada.reshape(1, n))


def _ffn_kernel(x_ref, mod_ref, wg_ref, wu_ref, wd_ref, *rest, shift_row, n_f, n_cast):
    cast_in = rest[:n_cast]
    o_ref = rest[n_cast]
    cast_out = rest[n_cast + 1:2 * n_cast + 1]
    h_ref = rest[2 * n_cast + 1]
    f = pl.program_id(1)
    tm = x_ref.shape[0]

    @pl.when(f == 0)
    def _():
        for r in range(tm // ROW_CHUNK):
            _rms_mod_rows(x_ref, mod_ref, h_ref, shift_row, pl.ds(r * ROW_CHUNK, ROW_CHUNK))
        o_ref[...] = jnp.zeros_like(o_ref)

    h = h_ref[...]
    g = jnp.dot(h, wg_ref[...], preferred_element_type=F32)
    u = jnp.dot(h, wu_ref[...], preferred_element_type=F32)
    act = (g * jax.nn.sigmoid(g) * u).astype(BF16)
    o_ref[...] += jnp.dot(act, wd_ref[...], preferred_element_type=F32)

    for src_ref, dst_ref in zip(cast_in, cast_out):
        dst_ref[...] = src_ref[...].astype(BF16)

    @pl.when(f == n_f - 1)
    def _():
        gate = FFN_RES * mod_ref[0, shift_row + 2:shift_row + 3, :]
        o_ref[...] = x_ref[...] + gate * o_ref[...]


def _cast_blocking(rows, cols, n_i, n_f):
    lane_groups = cols // 128
    n_c = max(k for k in range(1, n_f + 1) if lane_groups % k == 0)
    assert rows % (n_i * BF16_SUBLANE_TILE) == 0 and cols % 128 == 0
    return rows // n_i, cols // n_c, n_c


def _ffn(x2d, mod, w_gu, w_down, *, shift_row, seq, cast=(), layer=0):
    t, d = x2d.shape
    d_ff = w_down.shape[0]
    n_f = d_ff // FFN_TF
    tm = FFN_TM
    n_i = t // tm
    blockings = [_cast_blocking(w.shape[1], w.shape[2], n_i, n_f) for w in cast]

    def cast_map(n_c, lead):
        return lambda i, f: lead + (i, jnp.minimum(f, n_c - 1))

    cast_in_specs = [pl.BlockSpec((None, r, c), cast_map(n_c, (layer,))) for r, c, n_c in blockings]
    cast_out_specs = [pl.BlockSpec((r, c), cast_map(n_c, ())) for r, c, n_c in blockings]
    vmem = ((2 * tm * d * 4) * 2 + tm * d * 2 + 2 * 3 * d * FFN_TF * 2 + 6 * tm * FFN_TF * 4
            + sum(2 * r * c * 6 for r, c, _ in blockings) + (4 << 20))
    outs = pl.pallas_call(
        functools.partial(_ffn_kernel, shift_row=shift_row, n_f=n_f, n_cast=len(cast)),
        grid=(n_i, n_f),
        in_specs=[pl.BlockSpec((tm, d), lambda i, f: (i, 0)),
                  pl.BlockSpec((1, N_MOD, d), lambda i, f: (i * tm // seq, 0, 0)),
                  pl.BlockSpec((d, FFN_TF), lambda i, f: (0, f)),
                  pl.BlockSpec((d, FFN_TF), lambda i, f: (0, f + n_f)),
                  pl.BlockSpec((FFN_TF, d), lambda i, f: (f, 0))] + cast_in_specs,
        out_specs=[pl.BlockSpec((tm, d), lambda i, f: (i, 0))] + cast_out_specs,
        out_shape=[jax.ShapeDtypeStruct((t, d), F32)]
                  + [jax.ShapeDtypeStruct(w.shape[1:], BF16) for w in cast],
        scratch_shapes=[pltpu.VMEM((tm, d), BF16)],
        compiler_params=_params(("arbitrary", "arbitrary"), vmem),
        name="ffn",
    )(x2d, mod, w_gu, w_gu, w_down, *cast)
    return outs[0], outs[1:]


def _head_rms_gain(p, gain):
    outs = []
    for hd in range(p.shape[1] // HEAD_DIM):
        ph = p[:, hd * HEAD_DIM:(hd + 1) * HEAD_DIM]
        ms = jnp.mean(ph * ph, axis=-1, keepdims=True)
        outs.append(ph * lax.rsqrt(ms + EPS) * gain)
    return jnp.concatenate(outs, axis=1)


def _in_proj_kernel(x_ref, mod_ref, w_ref, qg_ref, kg_ref, o_ref, h_ref, *, shift_row, q_scale):
    n = pl.program_id(1)
    tm = x_ref.shape[0]

    @pl.when(n == 0)
    def _():
        for r in range(tm // ROW_CHUNK):
            _rms_mod_rows(x_ref, mod_ref, h_ref, shift_row, pl.ds(r * ROW_CHUNK, ROW_CHUNK))

    @pl.when(n < 2)
    def _():
        gain = jnp.where(n == 0, qg_ref[...] * q_scale, kg_ref[...])
        p = jnp.dot(h_ref[...], w_ref[...], preferred_element_type=F32)
        o_ref[...] = _head_rms_gain(p, gain).astype(BF16)

    @pl.when(n >= 2)
    def _():
        o_ref[...] = jnp.dot(h_ref[...], w_ref[...], preferred_element_type=F32).astype(BF16)


def _in_proj(x2d, mod, w_in, q_gain, k_gain, *, shift_row, seq):
    t, d = x2d.shape
    n_cols = w_in.shape[1]
    tm = PROJ_TM
    vmem = 2 * tm * d * 4 + tm * d * 2 + 2 * d * PROJ_TN * 2 + 2 * tm * PROJ_TN * 2 + 6 * tm * PROJ_TN * 4 + (4 << 20)
    q_scale = LOG2E / math.sqrt(HEAD_DIM)
    return pl.pallas_call(
        functools.partial(_in_proj_kernel, shift_row=shift_row, q_scale=q_scale),
        grid=(t // tm, n_cols // PROJ_TN),
        in_specs=[pl.BlockSpec((tm, d), lambda i, n: (i, 0)),
                  pl.BlockSpec((1, N_MOD, d), lambda i, n: (i * tm // seq, 0, 0)),
                  pl.BlockSpec((d, PROJ_TN), lambda i, n: (0, n)),
                  pl.BlockSpec((1, HEAD_DIM), lambda i, n: (0, 0)),
                  pl.BlockSpec((1, HEAD_DIM), lambda i, n: (0, 0))],
        out_specs=pl.BlockSpec((tm, PROJ_TN), lambda i, n: (i, n)),
        out_shape=jax.ShapeDtypeStruct((t, n_cols), BF16),
        scratch_shapes=[pltpu.VMEM((tm, d), BF16)],
        compiler_params=_params(("parallel", "arbitrary"), vmem),
        name="in_proj",
    )(x2d, mod, w_in, q_gain, k_gain)


def _attn_step(qs, ks, vs, neg_tri2, carries, causal_mask):
    n_h = len(qs)
    z, hilo, rs, neg_incl, a, pv = ({} for _ in range(6))

    def scores(h):
        z[h] = lax.dot_general(qs[h], ks[h], (((1,), (1,)), ((), ())), preferred_element_type=F32)

    def softplus(h):
        sp = jnp.maximum(z[h], 0.0) + jnp.log(1.0 + jnp.exp2(jnp.minimum(z[h], -z[h]))) * LOG2E
        if causal_mask is not None:
            sp = jnp.where(causal_mask, sp, 0.0)
        hi = sp.astype(BF16)
        lo = (sp - hi.astype(F32)).astype(BF16)
        hilo[h] = jnp.concatenate([hi, lo], axis=1)
        rs[h] = jnp.sum(sp, axis=-1, keepdims=True)

    def suffix_sum(h):
        neg_incl[h] = jnp.dot(hilo[h], neg_tri2, preferred_element_type=F32)

    def weights(h):
        w = jnp.exp2(z[h] + neg_incl[h] + carries[h])
        if causal_mask is not None:
            w = jnp.where(causal_mask, w, 0.0)
        a[h] = w.astype(BF16)

    def values(h):
        pv[h] = jnp.dot(a[h], vs[h], preferred_element_type=F32)

    stages = (scores, softplus, suffix_sum, weights, values)
    for wave in range(n_h + len(stages) - 1):
        for s in reversed(range(len(stages))):
            h = wave - s
            if 0 <= h < n_h:
                stages[s](h)
    return [pv[h] for h in range(n_h)], [rs[h] for h in range(n_h)]


def _attention_kernel(q_ref, k_ref, v_ref, o_ref):
    i = pl.program_id(2)
    t = q_ref.shape[0]
    n_h = q_ref.shape[1] // HEAD_DIM
    row = lax.broadcasted_iota(jnp.int32, (t, t), 0)
    col = lax.broadcasted_iota(jnp.int32, (t, t), 1)
    neg_tri = jnp.where(row >= col, -1.0, 0.0).astype(BF16)
    neg_tri2 = jnp.concatenate([neg_tri, neg_tri], axis=0)
    causal = col < row

    def lanes(hd):
        return slice(hd * HEAD_DIM, (hd + 1) * HEAD_DIM)

    qs = [q_ref[:, lanes(hd)] for hd in range(n_h)]

    def step(blk, accs, carries, mask):
        ks = [k_ref[blk, lanes(hd)] for hd in range(n_h)]
        vs = [v_ref[blk, lanes(hd)] for hd in range(n_h)]
        pvs, rss = _attn_step(qs, ks, vs, neg_tri2, carries, mask)
        accs = tuple(pvs) if accs is None else tuple(acc + pv for acc, pv in zip(accs, pvs))
        carries = tuple(cr - rs for cr, rs in zip(carries, rss))
        live = carries[0]
        for cr in carries[1:]:
            live = jnp.maximum(live, cr)
        return accs, carries, jnp.max(live)

    zero = jnp.zeros((t, 1), F32)
    accs, carries, live = step(pl.ds(pl.multiple_of(i * t, t), t), None, (zero,) * n_h, causal)

    def cond(state):
        jj, live = state[0], state[1]
        return jnp.logical_and(jj < i, live > F32_EXP2_UNDERFLOW)

    def body(state):
        jj, _, accs, carries = state
        blk = pl.ds(pl.multiple_of((i - 1 - jj) * t, t), t)
        accs, carries, live = step(blk, accs, carries, None)
        return jj + 1, live, accs, carries

    _, _, accs, _ = lax.while_loop(cond, body, (jnp.int32(0), live, accs, carries))
    for hd in range(n_h):
        o_ref[:, lanes(hd)] = accs[hd].astype(o_ref.dtype)


def _attention(proj, *, bsz, seq, n_heads):
    t = proj.shape[0]
    nq = seq // ATT_T
    group = ATT_HP * HEAD_DIM
    n_groups = n_heads // ATT_HP
    vmem = 2 * 2 * seq * group * 2 + ATT_HP * 16 * ATT_T * ATT_T * 4 + (4 << 20)
    return pl.pallas_call(
        _attention_kernel,
        grid=(bsz, n_groups, nq),
        in_specs=[pl.BlockSpec((ATT_T, group), lambda b, g, i: (b * nq + i, g)),
                  pl.BlockSpec((seq, group), lambda b, g, i: (b, n_groups + g)),
                  pl.BlockSpec((seq, group), lambda b, g, i: (b, 2 * n_groups + g))],
        out_specs=pl.BlockSpec((ATT_T, group), lambda b, g, i: (b * nq + i, g)),
        out_shape=jax.ShapeDtypeStruct((t, n_heads * HEAD_DIM), BF16),
        compiler_params=_params(("parallel", "parallel", "arbitrary"), vmem),
        name="attention",
    )(proj, proj, proj)


def _out_proj_kernel(x_ref, mod_ref, attn_ref, gb_ref, gc_ref, u_ref, gch_ref, uh_ref, cw_ref, w_ref,
                     o_ref, cu_ref, *, gate_row, seq):
    i = pl.program_id(0)
    tm = x_ref.shape[0]
    halo = gch_ref.shape[0]
    width = attn_ref.shape[1]

    mix = jnp.dot(attn_ref[...], w_ref[0:width, :], preferred_element_type=F32)

    prev = gch_ref[...].astype(F32) * uh_ref[...].astype(F32)
    seq_start = (i * tm) % seq == 0
    cu_ref[0:halo, :] = jnp.where(seq_start, 0.0, prev)
    cu_ref[halo:halo + tm, :] = gc_ref[...].astype(F32) * u_ref[...].astype(F32)

    conv = cu_ref[halo:halo + tm, :] * cw_ref[CONV_K - 1:CONV_K, :]
    for k in range(CONV_K - 1):
        back = CONV_K - 1 - k
        conv = conv + cu_ref[halo - back:halo - back + tm, :] * cw_ref[k:k + 1, :]
    y = (gb_ref[...].astype(F32) * conv).astype(BF16)

    mix = mix + jnp.dot(y, w_ref[width:, :], preferred_element_type=F32)
    gate = mod_ref[0, gate_row:gate_row + 1, :]
    o_ref[...] = x_ref[...] + gate * mix


def _out_proj(x2d, mod, attn, proj, conv_w, w_out, *, gate_row, seq):
    t, d = x2d.shape
    width = attn.shape[1]
    tm = OUT_TM
    halo = BF16_SUBLANE_TILE
    per = tm // halo
    b_col, c_col, u_col = 3, 4, 5

    def halo_map(col):
        return lambda i: (jnp.maximum(i * per - 1, 0), col)

    vmem = (2 * 2 * tm * d * 4 + 2 * 4 * tm * width * 2 + 2 * d * d * 2 + (tm + halo) * width * 4
            + 8 * tm * width * 4 + (4 << 20))
    return pl.pallas_call(
        functools.partial(_out_proj_kernel, gate_row=gate_row, seq=seq),
        grid=(t // tm,),
        in_specs=[pl.BlockSpec((tm, d), lambda i: (i, 0)),
                  pl.BlockSpec((1, N_MOD, d), lambda i: (i * tm // seq, 0, 0)),
                  pl.BlockSpec((tm, width), lambda i: (i, 0)),
                  pl.BlockSpec((tm, width), lambda i: (i, b_col)),
                  pl.BlockSpec((tm, width), lambda i: (i, c_col)),
                  pl.BlockSpec((tm, width), lambda i: (i, u_col)),
                  pl.BlockSpec((halo, width), halo_map(c_col)),
                  pl.BlockSpec((halo, width), halo_map(u_col)),
                  pl.BlockSpec((CONV_K, width), lambda i: (0, 0)),
                  pl.BlockSpec((d, d), lambda i: (0, 0))],
        out_specs=pl.BlockSpec((tm, d), lambda i: (i, 0)),
        out_shape=jax.ShapeDtypeStruct((t, d), F32),
        scratch_shapes=[pltpu.VMEM((tm + halo, width), F32)],
        compiler_params=_params(("parallel",), vmem),
        name="out_proj",
    )(x2d, mod, attn, proj, proj, proj, proj, proj, conv_w, w_out)


def kernel(x, c, w_ada, b_ada, w1_gu, w1_down, w_in, q_norm_w, k_norm_w, conv_w, w_out, w2_gu, w2_down):
    bsz, seq, d = x.shape
    depth = w_ada.shape[0]
    width = conv_w.shape[2]
    n_heads = width // HEAD_DIM
    assert w_in.shape[2] == 6 * width and n_heads % ATT_HP == 0 and width == PROJ_TN
    assert seq % max(FFN_TM, PROJ_TM, OUT_TM, ATT_T) == 0 and d % 128 == 0

    x2d = x.reshape(bsz * seq, d)
    for l in range(depth):
        mod = _adaln(c, w_ada[l], b_ada[l]).reshape(bsz, N_MOD, d)
        x2d, (w_in_b, w_out_b, w2_gu_b, w2_down_b) = _ffn(
            x2d, mod, w1_gu[l].astype(BF16), w1_down[l].astype(BF16), shift_row=0, seq=seq,
            cast=(w_in, w_out, w2_gu, w2_down), layer=l)
        proj = _in_proj(x2d, mod, w_in_b, q_norm_w[l].reshape(1, HEAD_DIM),
                        k_norm_w[l].reshape(1, HEAD_DIM), shift_row=3, seq=seq)
        attn = _attention(proj, bsz=bsz, seq=seq, n_heads=n_heads)
        x2d = _out_proj(x2d, mod, attn, proj, conv_w[l], w_out_b, gate_row=5, seq=seq)
        x2d, _ = _ffn(x2d, mod, w2_gu_b, w2_down_b, shift_row=6, seq=seq)
    return x2d.reshape(bsz, seq, d)
```

```python
import functools
import math

import jax
import jax.numpy as jnp
from jax import lax
from jax.experimental import pallas as pl
from jax.experimental.pallas import tpu as pltpu

F32 = jnp.float32
BF16 = jnp.bfloat16

HEAD_DIM = 128
CONV_K = 3
N_MOD = 9
FFN_RES = 0.5
EPS = 1e-6
LOG2E = 1.4426950408889634

F32_EXP2_UNDERFLOW = -150.0

V7X_VMEM_LIMIT_BYTES = 56 * 1024 * 1024
BF16_SUBLANE_TILE = 16

ADA_TN = 1024
FFN_TM = 512
FFN_TF = 512
PROJ_TM = 1024
PROJ_TN = 1024
ATT_T = 256
ATT_HP = 8
OUT_TM = 512
ROW_CHUNK = 256


def _params(semantics, vmem_bytes):
    return pltpu.CompilerParams(dimension_semantics=semantics,
                                vmem_limit_bytes=min(int(vmem_bytes), V7X_VMEM_LIMIT_BYTES))


def _rms_mod_rows(x_ref, mod_ref, h_ref, shift_row, rows):
    xv = x_ref[rows, :]
    ms = jnp.mean(xv * xv, axis=-1, keepdims=True)
    hn = xv * lax.rsqrt(ms + EPS)
    shift = mod_ref[0, shift_row:shift_row + 1, :]
    scale = mod_ref[0, shift_row + 1:shift_row + 2, :]
    h_ref[rows, :] = (hn * (1.0 + scale) + shift).astype(BF16)


def _adaln_kernel(c_ref, w_ref, b_ref, o_ref):
    cv = c_ref[...]
    c_act = (cv * jax.nn.sigmoid(cv)).astype(BF16)
    o_ref[...] = jnp.dot(c_act, w_ref[...].astype(BF16), preferred_element_type=F32) + b_ref[...]


def _adaln(c, w_ada, b_ada):
    bsz, d = c.shape
    n = w_ada.shape[1]
    return pl.pallas_call(
        _adaln_kernel,
        grid=(n // ADA_TN,),
        in_specs=[pl.BlockSpec((bsz, d), lambda j: (0, 0)),
                  pl.BlockSpec((d, ADA_TN), lambda j: (0, j)),
                  pl.BlockSpec((1, ADA_TN), lambda j: (0, j))],
        out_specs=pl.BlockSpec((bsz, ADA_TN), lambda j: (0, j)),
        out_shape=jax.ShapeDtypeStruct((bsz, n), F32),
        compiler_params=_params(("arbitrary",), 4 * d * ADA_TN * 4),
        name="adaln",
    )(c, w_ada, b_ada.reshape(1, n))


def _ffn_kernel(x_ref, mod_ref, wg_ref, wu_ref, wd_ref, *rest, shift_row, n_f, n_cast):
    cast_in = rest[:n_cast]
    o_ref = rest[n_cast]
    cast_out = rest[n_cast + 1:2 * n_cast + 1]
    h_ref = rest[2 * n_cast + 1]
    f = pl.program_id(1)
    tm = x_ref.shape[0]

    def step(first, last):
        if first:
            for r in range(tm // ROW_CHUNK):
                _rms_mod_rows(x_ref, mod_ref, h_ref, shift_row, pl.ds(r * ROW_CHUNK, ROW_CHUNK))
        h = h_ref[...]
        g = jnp.dot(h, wg_ref[...], preferred_element_type=F32)
        u = jnp.dot(h, wu_ref[...], preferred_element_type=F32)
        act = (g * jax.nn.sigmoid(g) * u).astype(BF16)
        part = jnp.dot(act, wd_ref[...], preferred_element_type=F32)
        if first:
            o_ref[...] = part
        elif last:
            gate = FFN_RES * mod_ref[0, shift_row + 2:shift_row + 3, :]
            o_ref[...] = x_ref[...] + gate * (o_ref[...] + part)
        else:
            o_ref[...] += part
        for src_ref, dst_ref in zip(cast_in, cast_out):
            dst_ref[...] = src_ref[...].astype(BF16)

    pl.when(f == 0)(functools.partial(step, True, False))
    pl.when(jnp.logical_and(f > 0, f < n_f - 1))(functools.partial(step, False, False))
    pl.when(f == n_f - 1)(functools.partial(step, False, True))


def _cast_blocking(rows, cols, n_i, n_f):
    lane_groups = cols // 128
    n_c = max(k for k in range(1, n_f + 1) if lane_groups % k == 0)
    assert rows % (n_i * BF16_SUBLANE_TILE) == 0 and cols % 128 == 0
    return rows // n_i, cols // n_c, n_c


def _ffn(x2d, mod, w_gu, w_down, *, shift_row, seq, cast=(), layer=0):
    t, d = x2d.shape
    d_ff = w_down.shape[0]
    n_f = d_ff // FFN_TF
    assert n_f >= 3
    tm = FFN_TM
    n_i = t // tm
    blockings = [_cast_blocking(w.shape[1], w.shape[2], n_i, n_f) for w in cast]

    def cast_map(n_c, lead):
        return lambda i, f: lead + (i, jnp.minimum(f, n_c - 1))

    cast_in_specs = [pl.BlockSpec((None, r, c), cast_map(n_c, (layer,))) for r, c, n_c in blockings]
    cast_out_specs = [pl.BlockSpec((r, c), cast_map(n_c, ())) for r, c, n_c in blockings]
    vmem = ((2 * tm * d * 4) * 2 + tm * d * 2 + 2 * 3 * d * FFN_TF * 2 + 6 * tm * FFN_TF * 4
            + sum(2 * r * c * 6 for r, c, _ in blockings) + (4 << 20))
    outs = pl.pallas_call(
        functools.partial(_ffn_kernel, shift_row=shift_row, n_f=n_f, n_cast=len(cast)),
        grid=(n_i, n_f),
        in_specs=[pl.BlockSpec((tm, d), lambda i, f: (i, 0)),
                  pl.BlockSpec((1, N_MOD, d), lambda i, f: (i * tm // seq, 0, 0)),
                  pl.BlockSpec((d, FFN_TF), lambda i, f: (0, f)),
                  pl.BlockSpec((d, FFN_TF), lambda i, f: (0, f + n_f)),
                  pl.BlockSpec((FFN_TF, d), lambda i, f: (f, 0))] + cast_in_specs,
        out_specs=[pl.BlockSpec((tm, d), lambda i, f: (i, 0))] + cast_out_specs,
        out_shape=[jax.ShapeDtypeStruct((t, d), F32)]
                  + [jax.ShapeDtypeStruct(w.shape[1:], BF16) for w in cast],
        scratch_shapes=[pltpu.VMEM((tm, d), BF16)],
        compiler_params=_params(("arbitrary", "arbitrary"), vmem),
        name="ffn",
    )(x2d, mod, w_gu, w_gu, w_down, *cast)
    return outs[0], outs[1:]


def _head_rms_gain(p, gain):
    outs = []
    for hd in range(p.shape[1] // HEAD_DIM):
        ph = p[:, hd * HEAD_DIM:(hd + 1) * HEAD_DIM]
        ms = jnp.mean(ph * ph, axis=-1, keepdims=True)
        outs.append(ph * lax.rsqrt(ms + EPS) * gain)
    return jnp.concatenate(outs, axis=1)


def _in_proj_kernel(x_ref, mod_ref, w_ref, qg_ref, kg_ref, o_ref, h_ref, *, shift_row, q_scale):
    n = pl.program_id(1)
    tm = x_ref.shape[0]

    @pl.when(n == 0)
    def _():
        for r in range(tm // ROW_CHUNK):
            _rms_mod_rows(x_ref, mod_ref, h_ref, shift_row, pl.ds(r * ROW_CHUNK, ROW_CHUNK))

    @pl.when(n < 2)
    def _():
        gain = jnp.where(n == 0, qg_ref[...] * q_scale, kg_ref[...])
        p = jnp.dot(h_ref[...], w_ref[...], preferred_element_type=F32)
        o_ref[...] = _head_rms_gain(p, gain).astype(BF16)

    @pl.when(n >= 2)
    def _():
        o_ref[...] = jnp.dot(h_ref[...], w_ref[...], preferred_element_type=F32).astype(BF16)


def _in_proj(x2d, mod, w_in, q_gain, k_gain, *, shift_row, seq):
    t, d = x2d.shape
    n_cols = w_in.shape[1]
    tm = PROJ_TM
    vmem = 2 * tm * d * 4 + tm * d * 2 + 2 * d * PROJ_TN * 2 + 2 * tm * PROJ_TN * 2 + 6 * tm * PROJ_TN * 4 + (4 << 20)
    q_scale = LOG2E / math.sqrt(HEAD_DIM)
    return pl.pallas_call(
        functools.partial(_in_proj_kernel, shift_row=shift_row, q_scale=q_scale),
        grid=(t // tm, n_cols // PROJ_TN),
        in_specs=[pl.BlockSpec((tm, d), lambda i, n: (i, 0)),
                  pl.BlockSpec((1, N_MOD, d), lambda i, n: (i * tm // seq, 0, 0)),
                  pl.BlockSpec((d, PROJ_TN), lambda i, n: (0, n)),
                  pl.BlockSpec((1, HEAD_DIM), lambda i, n: (0, 0)),
                  pl.BlockSpec((1, HEAD_DIM), lambda i, n: (0, 0))],
        out_specs=pl.BlockSpec((tm, PROJ_TN), lambda i, n: (i, n)),
        out_shape=jax.ShapeDtypeStruct((t, n_cols), BF16),
        scratch_shapes=[pltpu.VMEM((tm, d), BF16)],
        compiler_params=_params(("parallel", "arbitrary"), vmem),
        name="in_proj",
    )(x2d, mod, w_in, q_gain, k_gain)


def _attn_step(qs, ks, vs, neg_tri, carries, causal_mask):
    n_h = len(qs)
    z, sp16, rs, neg_incl, a, pv = ({} for _ in range(6))

    def scores(h):
        z[h] = lax.dot_general(qs[h], ks[h], (((1,), (1,)), ((), ())), preferred_element_type=F32)

    def softplus(h):
        sp = jnp.maximum(z[h], 0.0) + jnp.log(1.0 + jnp.exp2(jnp.minimum(z[h], -z[h]))) * LOG2E
        if causal_mask is not None:
            sp = jnp.where(causal_mask, sp, 0.0)
        sp16[h] = sp.astype(BF16)
        rs[h] = jnp.sum(sp, axis=-1, keepdims=True)

    def suffix_sum(h):
        neg_incl[h] = jnp.dot(sp16[h], neg_tri, preferred_element_type=F32)

    def weights(h):
        w = jnp.exp2(z[h] + neg_incl[h] + carries[h])
        if causal_mask is not None:
            w = jnp.where(causal_mask, w, 0.0)
        a[h] = w.astype(BF16)

    def values(h):
        pv[h] = jnp.dot(a[h], vs[h], preferred_element_type=F32)

    stages = (scores, softplus, suffix_sum, weights, values)
    for wave in range(n_h + len(stages) - 1):
        for s in reversed(range(len(stages))):
            h = wave - s
            if 0 <= h < n_h:
                stages[s](h)
    return [pv[h] for h in range(n_h)], [rs[h] for h in range(n_h)]


def _attention_kernel(q_ref, k_ref, v_ref, o_ref):
    i = pl.program_id(2)
    t = q_ref.shape[0]
    n_h = q_ref.shape[1] // HEAD_DIM
    row = lax.broadcasted_iota(jnp.int32, (t, t), 0)
    col = lax.broadcasted_iota(jnp.int32, (t, t), 1)
    neg_tri = jnp.where(row >= col, -1.0, 0.0).astype(BF16)
    causal = col < row

    def lanes(hd):
        return slice(hd * HEAD_DIM, (hd + 1) * HEAD_DIM)

    qs = [q_ref[:, lanes(hd)] for hd in range(n_h)]

    def step(blk, accs, carries, mask):
        ks = [k_ref[blk, lanes(hd)] for hd in range(n_h)]
        vs = [v_ref[blk, lanes(hd)] for hd in range(n_h)]
        pvs, rss = _attn_step(qs, ks, vs, neg_tri, carries, mask)
        accs = tuple(pvs) if accs is None else tuple(acc + pv for acc, pv in zip(accs, pvs))
        carries = tuple(cr - rs for cr, rs in zip(carries, rss))
        live = carries[0]
        for cr in carries[1:]:
            live = jnp.maximum(live, cr)
        return accs, carries, jnp.max(live)

    zero = jnp.zeros((t, 1), F32)
    accs, carries, live = step(pl.ds(pl.multiple_of(i * t, t), t), None, (zero,) * n_h, causal)

    def cond(state):
        jj, live = state[0], state[1]
        return jnp.logical_and(jj < i, live > F32_EXP2_UNDERFLOW)

    def body(state):
        jj, _, accs, carries = state
        blk = pl.ds(pl.multiple_of((i - 1 - jj) * t, t), t)
        accs, carries, live = step(blk, accs, carries, None)
        return jj + 1, live, accs, carries

    _, _, accs, _ = lax.while_loop(cond, body, (jnp.int32(0), live, accs, carries))
    for hd in range(n_h):
        o_ref[:, lanes(hd)] = accs[hd].astype(o_ref.dtype)


def _attention(proj, *, bsz, seq, n_heads):
    t = proj.shape[0]
    nq = seq // ATT_T
    group = ATT_HP * HEAD_DIM
    n_groups = n_heads // ATT_HP
    vmem = 2 * 2 * seq * group * 2 + ATT_HP * 16 * ATT_T * ATT_T * 4 + (4 << 20)
    return pl.pallas_call(
        _attention_kernel,
        grid=(bsz, n_groups, nq),
        in_specs=[pl.BlockSpec((ATT_T, group), lambda b, g, i: (b * nq + i, g)),
                  pl.BlockSpec((seq, group), lambda b, g, i: (b, n_groups + g)),
                  pl.BlockSpec((seq, group), lambda b, g, i: (b, 2 * n_groups + g))],
        out_specs=pl.BlockSpec((ATT_T, group), lambda b, g, i: (b * nq + i, g)),
        out_shape=jax.ShapeDtypeStruct((t, n_heads * HEAD_DIM), BF16),
        compiler_params=_params(("parallel", "parallel", "arbitrary"), vmem),
        name="attention",
    )(proj, proj, proj)


def _out_proj_kernel(x_ref, mod_ref, attn_ref, gb_ref, gc_ref, u_ref, gch_ref, uh_ref, cw_ref, w_ref,
                     o_ref, cu_ref, *, gate_row, seq):
    i = pl.program_id(0)
    tm = x_ref.shape[0]
    halo = gch_ref.shape[0]
    width = attn_ref.shape[1]

    mix = jnp.dot(attn_ref[...], w_ref[0:width, :], preferred_element_type=F32)

    prev = gch_ref[...].astype(F32) * uh_ref[...].astype(F32)
    seq_start = (i * tm) % seq == 0
    cu_ref[0:halo, :] = jnp.where(seq_start, 0.0, prev)
    cu_ref[halo:halo + tm, :] = gc_ref[...].astype(F32) * u_ref[...].astype(F32)

    conv = cu_ref[halo:halo + tm, :] * cw_ref[CONV_K - 1:CONV_K, :]
    for k in range(CONV_K - 1):
        back = CONV_K - 1 - k
        conv = conv + cu_ref[halo - back:halo - back + tm, :] * cw_ref[k:k + 1, :]
    y = (gb_ref[...].astype(F32) * conv).astype(BF16)

    mix = mix + jnp.dot(y, w_ref[width:, :], preferred_element_type=F32)
    gate = mod_ref[0, gate_row:gate_row + 1, :]
    o_ref[...] = x_ref[...] + gate * mix


def _out_proj(x2d, mod, attn, proj, conv_w, w_out, *, gate_row, seq):
    t, d = x2d.shape
    width = attn.shape[1]
    tm = OUT_TM
    halo = BF16_SUBLANE_TILE
    per = tm // halo
    b_col, c_col, u_col = 3, 4, 5

    def halo_map(col):
        return lambda i: (jnp.maximum(i * per - 1, 0), col)

    vmem = (2 * 2 * tm * d * 4 + 2 * 4 * tm * width * 2 + 2 * d * d * 2 + (tm + halo) * width * 4
            + 8 * tm * width * 4 + (4 << 20))
    return pl.pallas_call(
        functools.partial(_out_proj_kernel, gate_row=gate_row, seq=seq),
        grid=(t // tm,),
        in_specs=[pl.BlockSpec((tm, d), lambda i: (i, 0)),
                  pl.BlockSpec((1, N_MOD, d), lambda i: (i * tm // seq, 0, 0)),
                  pl.BlockSpec((tm, width), lambda i: (i, 0)),
                  pl.BlockSpec((tm, width), lambda i: (i, b_col)),
                  pl.BlockSpec((tm, width), lambda i: (i, c_col)),
                  pl.BlockSpec((tm, width), lambda i: (i, u_col)),
                  pl.BlockSpec((halo, width), halo_map(c_col)),
                  pl.BlockSpec((halo, width), halo_map(u_col)),
                  pl.BlockSpec((CONV_K, width), lambda i: (0, 0)),
                  pl.BlockSpec((d, d), lambda i: (0, 0))],
        out_specs=pl.BlockSpec((tm, d), lambda i: (i, 0)),
        out_shape=jax.ShapeDtypeStruct((t, d), F32),
        scratch_shapes=[pltpu.VMEM((tm + halo, width), F32)],
        compiler_params=_params(("parallel",), vmem),
        name="out_proj",
    )(x2d, mod, attn, proj, proj, proj, proj, proj, conv_w, w_out)


def kernel(x, c, w_ada, b_ada, w1_gu, w1_down, w_in, q_norm_w, k_norm_w, conv_w, w_out, w2_gu, w2_down):
    bsz, seq, d = x.shape
    depth = w_ada.shape[0]
    width = conv_w.shape[2]
    n_heads = width // HEAD_DIM
    assert w_in.shape[2] == 6 * width and n_heads % ATT_HP == 0 and width == PROJ_TN
    assert seq % max(FFN_TM, PROJ_TM, OUT_TM, ATT_T) == 0 and d % 128 == 0

    x2d = x.reshape(bsz * seq, d)
    for l in range(depth):
        mod = _adaln(c, w_ada[l], b_ada[l]).reshape(bsz, N_MOD, d)
        x2d, (w_in_b, w_out_b, w2_gu_b, w2_down_b) = _ffn(
            x2d, mod, w1_gu[l].astype(BF16), w1_down[l].astype(BF16), shift_row=0, seq=seq,
            cast=(w_in, w_out, w2_gu, w2_down), layer=l)
        proj = _in_proj(x2d, mod, w_in_b, q_norm_w[l].reshape(1, HEAD_DIM),
                        k_norm_w[l].reshape(1, HEAD_DIM), shift_row=3, seq=seq)
        attn = _attention(proj, bsz=bsz, seq=seq, n_heads=n_heads)
        x2d = _out_proj(x2d, mod, attn, proj, conv_w[l], w_out_b, gate_row=5, seq=seq)
        x2d, _ = _ffn(x2d, mod, w2_gu_b, w2_down_b, shift_row=6, seq=seq)
    return x2d.reshape(bsz, seq, d)
```

```python
import functools
import math

import jax
import jax.numpy as jnp
from jax import lax
from jax.experimental import pallas as pl
from jax.experimental.pallas import tpu as pltpu

F32 = jnp.float32
BF16 = jnp.bfloat16

HEAD_DIM = 128
CONV_K = 3
N_MOD = 9
FFN_RES = 0.5
EPS = 1e-6
LOG2E = 1.4426950408889634

F32_EXP2_UNDERFLOW = -150.0

V7X_VMEM_LIMIT_BYTES = 58 * 1024 * 1024
BF16_SUBLANE_TILE = 16

ADA_TN = 1024
FFN_TM = 1024
FFN_TF = 512
PROJ_TM = 1024
PROJ_TN = 1024
ATT_T = 256
ATT_HP = 8
OUT_TM = 512
ROW_CHUNK = 256


def _params(semantics, vmem_bytes):
    return pltpu.CompilerParams(dimension_semantics=semantics,
                                vmem_limit_bytes=min(int(vmem_bytes), V7X_VMEM_LIMIT_BYTES))


def _rms_mod_rows(x_ref, mod_ref, h_ref, shift_row, rows):
    xv = x_ref[rows, :]
    ms = jnp.mean(xv * xv, axis=-1, keepdims=True)
    hn = xv * lax.rsqrt(ms + EPS)
    shift = mod_ref[0, shift_row:shift_row + 1, :]
    scale = mod_ref[0, shift_row + 1:shift_row + 2, :]
    h_ref[rows, :] = (hn * (1.0 + scale) + shift).astype(BF16)


def _adaln_kernel(c_ref, w_ref, b_ref, o_ref):
    cv = c_ref[...]
    c_act = (cv * jax.nn.sigmoid(cv)).astype(BF16)
    o_ref[...] = jnp.dot(c_act, w_ref[...].astype(BF16), preferred_element_type=F32) + b_ref[...]


def _adaln(c, w_ada, b_ada):
    bsz, d = c.shape
    n = w_ada.shape[1]
    return pl.pallas_call(
        _adaln_kernel,
        grid=(n // ADA_TN,),
        in_specs=[pl.BlockSpec((bsz, d), lambda j: (0, 0)),
                  pl.BlockSpec((d, ADA_TN), lambda j: (0, j)),
                  pl.BlockSpec((1, ADA_TN), lambda j: (0, j))],
        out_specs=pl.BlockSpec((bsz, ADA_TN), lambda j: (0, j)),
        out_shape=jax.ShapeDtypeStruct((bsz, n), F32),
        compiler_params=_params(("arbitrary",), 4 * d * ADA_TN * 4),
        name="adaln",
    )(c, w_ada, b_ada.reshape(1, n))


def _cast_blocking(rows, cols, n_i, n_j):
    lane_groups = cols // 128
    n_c = max(k for k in range(1, n_j + 1) if lane_groups % k == 0)
    assert rows % (n_i * BF16_SUBLANE_TILE) == 0 and cols % 128 == 0
    return rows // n_i, cols // n_c, n_c


def _side_cast_specs(cast, layer, n_i, n_j):
    blockings = [_cast_blocking(w.shape[1], w.shape[2], n_i, n_j) for w in cast]

    def cast_map(n_c, lead):
        return lambda i, j: lead + (i, jnp.minimum(j, n_c - 1))

    in_specs = [pl.BlockSpec((None, r, c), cast_map(n_c, (layer,))) for r, c, n_c in blockings]
    out_specs = [pl.BlockSpec((r, c), cast_map(n_c, ())) for r, c, n_c in blockings]
    out_shapes = [jax.ShapeDtypeStruct(w.shape[1:], BF16) for w in cast]
    return in_specs, out_specs, out_shapes, sum(2 * r * c * 6 for r, c, _ in blockings)


def _split_cast_refs(rest, n_cast):
    return rest[:n_cast], rest[n_cast], rest[n_cast + 1:2 * n_cast + 1], rest[2 * n_cast + 1:]


def _side_cast(cast_in, cast_out):
    for src_ref, dst_ref in zip(cast_in, cast_out):
        dst_ref[...] = src_ref[...].astype(BF16)


def _ffn_kernel(x_ref, mod_ref, wg_ref, wu_ref, wd_ref, *rest, shift_row, n_f, n_cast):
    cast_in, o_ref, cast_out, (h_ref,) = _split_cast_refs(rest, n_cast)
    f = pl.program_id(1)
    tm = x_ref.shape[0]

    def step(first, last):
        if first:
            for r in range(tm // ROW_CHUNK):
                _rms_mod_rows(x_ref, mod_ref, h_ref, shift_row, pl.ds(r * ROW_CHUNK, ROW_CHUNK))
        h = h_ref[...]
        g = jnp.dot(h, wg_ref[...], preferred_element_type=F32)
        u = jnp.dot(h, wu_ref[...], preferred_element_type=F32)
        act = (g * jax.nn.sigmoid(g) * u).astype(BF16)
        part = jnp.dot(act, wd_ref[...], preferred_element_type=F32)
        if first:
            o_ref[...] = part
        elif last:
            gate = FFN_RES * mod_ref[0, shift_row + 2:shift_row + 3, :]
            o_ref[...] = x_ref[...] + gate * (o_ref[...] + part)
        else:
            o_ref[...] += part
        _side_cast(cast_in, cast_out)

    pl.when(f == 0)(functools.partial(step, True, False))
    pl.when(jnp.logical_and(f > 0, f < n_f - 1))(functools.partial(step, False, False))
    pl.when(f == n_f - 1)(functools.partial(step, False, True))


def _ffn(x2d, mod, w_gu, w_down, *, shift_row, seq, cast=(), layer=0):
    t, d = x2d.shape
    d_ff = w_down.shape[0]
    n_f = d_ff // FFN_TF
    assert n_f >= 3
    tm = FFN_TM
    n_i = t // tm
    cast_in_specs, cast_out_specs, cast_shapes, cast_vmem = _side_cast_specs(cast, layer, n_i, n_f)
    vmem = ((2 * tm * d * 4) * 2 + tm * d * 2 + 2 * 3 * d * FFN_TF * 2 + 6 * tm * FFN_TF * 4
            + cast_vmem + (4 << 20))
    outs = pl.pallas_call(
        functools.partial(_ffn_kernel, shift_row=shift_row, n_f=n_f, n_cast=len(cast)),
        grid=(n_i, n_f),
        in_specs=[pl.BlockSpec((tm, d), lambda i, f: (i, 0)),
                  pl.BlockSpec((1, N_MOD, d), lambda i, f: (i * tm // seq, 0, 0)),
                  pl.BlockSpec((d, FFN_TF), lambda i, f: (0, f)),
                  pl.BlockSpec((d, FFN_TF), lambda i, f: (0, f + n_f)),
                  pl.BlockSpec((FFN_TF, d), lambda i, f: (f, 0))] + cast_in_specs,
        out_specs=[pl.BlockSpec((tm, d), lambda i, f: (i, 0))] + cast_out_specs,
        out_shape=[jax.ShapeDtypeStruct((t, d), F32)] + cast_shapes,
        scratch_shapes=[pltpu.VMEM((tm, d), BF16)],
        compiler_params=_params(("arbitrary", "arbitrary"), vmem),
        name="ffn",
    )(x2d, mod, w_gu, w_gu, w_down, *cast)
    return outs[0], outs[1:]


def _head_rms_gain(p, gain):
    outs = []
    for hd in range(p.shape[1] // HEAD_DIM):
        ph = p[:, hd * HEAD_DIM:(hd + 1) * HEAD_DIM]
        ms = jnp.mean(ph * ph, axis=-1, keepdims=True)
        outs.append(ph * lax.rsqrt(ms + EPS) * gain)
    return jnp.concatenate(outs, axis=1)


def _in_proj_kernel(x_ref, mod_ref, w_ref, qg_ref, kg_ref, *rest, shift_row, q_scale, n_cast):
    cast_in, o_ref, cast_out, (h_ref,) = _split_cast_refs(rest, n_cast)
    n = pl.program_id(1)
    tm = x_ref.shape[0]

    @pl.when(n == 0)
    def _():
        for r in range(tm // ROW_CHUNK):
            _rms_mod_rows(x_ref, mod_ref, h_ref, shift_row, pl.ds(r * ROW_CHUNK, ROW_CHUNK))
        p = jnp.dot(h_ref[...], w_ref[...], preferred_element_type=F32)
        o_ref[...] = _head_rms_gain(p, qg_ref[...] * q_scale).astype(BF16)
        _side_cast(cast_in, cast_out)

    @pl.when(n == 1)
    def _():
        p = jnp.dot(h_ref[...], w_ref[...], preferred_element_type=F32)
        o_ref[...] = _head_rms_gain(p, kg_ref[...]).astype(BF16)
        _side_cast(cast_in, cast_out)

    @pl.when(n >= 2)
    def _():
        o_ref[...] = jnp.dot(h_ref[...], w_ref[...], preferred_element_type=F32).astype(BF16)
        _side_cast(cast_in, cast_out)


def _in_proj(x2d, mod, w_in, q_gain, k_gain, *, shift_row, seq, cast=(), layer=0):
    t, d = x2d.shape
    n_cols = w_in.shape[1]
    tm = PROJ_TM
    n_i, n_n = t // tm, n_cols // PROJ_TN
    cast_in_specs, cast_out_specs, cast_shapes, cast_vmem = _side_cast_specs(cast, layer, n_i, n_n)
    vmem = (2 * tm * d * 4 + tm * d * 2 + 2 * d * PROJ_TN * 2 + 2 * tm * PROJ_TN * 2 + 6 * tm * PROJ_TN * 4
            + cast_vmem + (4 << 20))
    q_scale = LOG2E / math.sqrt(HEAD_DIM)
    outs = pl.pallas_call(
        functools.partial(_in_proj_kernel, shift_row=shift_row, q_scale=q_scale, n_cast=len(cast)),
        grid=(n_i, n_n),
        in_specs=[pl.BlockSpec((tm, d), lambda i, n: (i, 0)),
                  pl.BlockSpec((1, N_MOD, d), lambda i, n: (i * tm // seq, 0, 0)),
                  pl.BlockSpec((d, PROJ_TN), lambda i, n: (0, n)),
                  pl.BlockSpec((1, HEAD_DIM), lambda i, n: (0, 0)),
                  pl.BlockSpec((1, HEAD_DIM), lambda i, n: (0, 0))] + cast_in_specs,
        out_specs=[pl.BlockSpec((tm, PROJ_TN), lambda i, n: (i, n))] + cast_out_specs,
        out_shape=[jax.ShapeDtypeStruct((t, n_cols), BF16)] + cast_shapes,
        scratch_shapes=[pltpu.VMEM((tm, d), BF16)],
        compiler_params=_params(("arbitrary", "arbitrary"), vmem),
        name="in_proj",
    )(x2d, mod, w_in, q_gain, k_gain, *cast)
    return outs[0], outs[1:]


def _attn_step(qs, ks, vs, neg_tri, carries, causal_mask):
    n_h = len(qs)
    z, sp16, rs, neg_incl, a, pv = ({} for _ in range(6))

    def scores(h):
        z[h] = lax.dot_general(qs[h], ks[h], (((1,), (1,)), ((), ())), preferred_element_type=F32)

    def softplus(h):
        sp = jnp.maximum(z[h], 0.0) + jnp.log(1.0 + jnp.exp2(jnp.minimum(z[h], -z[h]))) * LOG2E
        if causal_mask is not None:
            sp = jnp.where(causal_mask, sp, 0.0)
        sp16[h] = sp.astype(BF16)
        rs[h] = jnp.sum(sp, axis=-1, keepdims=True)

    def suffix_sum(h):
        neg_incl[h] = jnp.dot(sp16[h], neg_tri, preferred_element_type=F32)

    def weights(h):
        w = jnp.exp2(z[h] + neg_incl[h] + carries[h])
        if causal_mask is not None:
            w = jnp.where(causal_mask, w, 0.0)
        a[h] = w.astype(BF16)

    def values(h):
        pv[h] = jnp.dot(a[h], vs[h], preferred_element_type=F32)

    stages = (scores, softplus, suffix_sum, weights, values)
    for wave in range(n_h + len(stages) - 1):
        for s in reversed(range(len(stages))):
            h = wave - s
            if 0 <= h < n_h:
                stages[s](h)
    return [pv[h] for h in range(n_h)], [rs[h] for h in range(n_h)]


def _attention_kernel(q_ref, k_ref, v_ref, o_ref):
    i = pl.program_id(2)
    t = q_ref.shape[0]
    n_h = q_ref.shape[1] // HEAD_DIM
    row = lax.broadcasted_iota(jnp.int32, (t, t), 0)
    col = lax.broadcasted_iota(jnp.int32, (t, t), 1)
    neg_tri = jnp.where(row >= col, -1.0, 0.0).astype(BF16)
    causal = col < row

    def lanes(hd):
        return slice(hd * HEAD_DIM, (hd + 1) * HEAD_DIM)

    qs = [q_ref[:, lanes(hd)] for hd in range(n_h)]

    def step(blk, accs, carries, mask):
        ks = [k_ref[blk, lanes(hd)] for hd in range(n_h)]
        vs = [v_ref[blk, lanes(hd)] for hd in range(n_h)]
        pvs, rss = _attn_step(qs, ks, vs, neg_tri, carries, mask)
        accs = tuple(pvs) if accs is None else tuple(acc + pv for acc, pv in zip(accs, pvs))
        carries = tuple(cr - rs for cr, rs in zip(carries, rss))
        live = carries[0]
        for cr in carries[1:]:
            live = jnp.maximum(live, cr)
        return accs, carries, jnp.max(live)

    zero = jnp.zeros((t, 1), F32)
    accs, carries, live = step(pl.ds(pl.multiple_of(i * t, t), t), None, (zero,) * n_h, causal)

    def cond(state):
        jj, live = state[0], state[1]
        return jnp.logical_and(jj < i, live > F32_EXP2_UNDERFLOW)

    def body(state):
        jj, _, accs, carries = state
        blk = pl.ds(pl.multiple_of((i - 1 - jj) * t, t), t)
        accs, carries, live = step(blk, accs, carries, None)
        return jj + 1, live, accs, carries

    _, _, accs, _ = lax.while_loop(cond, body, (jnp.int32(0), live, accs, carries))
    for hd in range(n_h):
        o_ref[:, lanes(hd)] = accs[hd].astype(o_ref.dtype)


def _attention(proj, *, bsz, seq, n_heads):
    t = proj.shape[0]
    nq = seq // ATT_T
    group = ATT_HP * HEAD_DIM
    n_groups = n_heads // ATT_HP
    vmem = 2 * 2 * seq * group * 2 + ATT_HP * 16 * ATT_T * ATT_T * 4 + (4 << 20)
    return pl.pallas_call(
        _attention_kernel,
        grid=(bsz, n_groups, nq),
        in_specs=[pl.BlockSpec((ATT_T, group), lambda b, g, i: (b * nq + i, g)),
                  pl.BlockSpec((seq, group), lambda b, g, i: (b, n_groups + g)),
                  pl.BlockSpec((seq, group), lambda b, g, i: (b, 2 * n_groups + g))],
        out_specs=pl.BlockSpec((ATT_T, group), lambda b, g, i: (b * nq + i, g)),
        out_shape=jax.ShapeDtypeStruct((t, n_heads * HEAD_DIM), BF16),
        compiler_params=_params(("parallel", "parallel", "arbitrary"), vmem),
        name="attention",
    )(proj, proj, proj)


def _out_proj_kernel(x_ref, mod_ref, attn_ref, gb_ref, gc_ref, u_ref, gch_ref, uh_ref, cw_ref, w_ref,
                     o_ref, cu_ref, *, gate_row, seq):
    i = pl.program_id(0)
    tm = x_ref.shape[0]
    halo = gch_ref.shape[0]
    width = attn_ref.shape[1]

    mix = jnp.dot(attn_ref[...], w_ref[0:width, :], preferred_element_type=F32)

    prev = gch_ref[...].astype(F32) * uh_ref[...].astype(F32)
    seq_start = (i * tm) % seq == 0
    cu_ref[0:halo, :] = jnp.where(seq_start, 0.0, prev)
    cu_ref[halo:halo + tm, :] = gc_ref[...].astype(F32) * u_ref[...].astype(F32)

    conv = cu_ref[halo:halo + tm, :] * cw_ref[CONV_K - 1:CONV_K, :]
    for k in range(CONV_K - 1):
        back = CONV_K - 1 - k
        conv = conv + cu_ref[halo - back:halo - back + tm, :] * cw_ref[k:k + 1, :]
    y = (gb_ref[...].astype(F32) * conv).astype(BF16)

    mix = mix + jnp.dot(y, w_ref[width:, :], preferred_element_type=F32)
    gate = mod_ref[0, gate_row:gate_row + 1, :]
    o_ref[...] = x_ref[...] + gate * mix


def _out_proj(x2d, mod, attn, proj, conv_w, w_out, *, gate_row, seq):
    t, d = x2d.shape
    width = attn.shape[1]
    tm = OUT_TM
    halo = BF16_SUBLANE_TILE
    per = tm // halo
    b_col, c_col, u_col = 3, 4, 5

    def halo_map(col):
        return lambda i: (jnp.maximum(i * per - 1, 0), col)

    vmem = (2 * 2 * tm * d * 4 + 2 * 4 * tm * width * 2 + 2 * d * d * 2 + (tm + halo) * width * 4
            + 8 * tm * width * 4 + (4 << 20))
    return pl.pallas_call(
        functools.partial(_out_proj_kernel, gate_row=gate_row, seq=seq),
        grid=(t // tm,),
        in_specs=[pl.BlockSpec((tm, d), lambda i: (i, 0)),
                  pl.BlockSpec((1, N_MOD, d), lambda i: (i * tm // seq, 0, 0)),
                  pl.BlockSpec((tm, width), lambda i: (i, 0)),
                  pl.BlockSpec((tm, width), lambda i: (i, b_col)),
                  pl.BlockSpec((tm, width), lambda i: (i, c_col)),
                  pl.BlockSpec((tm, width), lambda i: (i, u_col)),
                  pl.BlockSpec((halo, width), halo_map(c_col)),
                  pl.BlockSpec((halo, width), halo_map(u_col)),
                  pl.BlockSpec((CONV_K, width), lambda i: (0, 0)),
                  pl.BlockSpec((d, d), lambda i: (0, 0))],
        out_specs=pl.BlockSpec((tm, d), lambda i: (i, 0)),
        out_shape=jax.ShapeDtypeStruct((t, d), F32),
        scratch_shapes=[pltpu.VMEM((tm + halo, width), F32)],
        compiler_params=_params(("parallel",), vmem),
        name="out_proj",
    )(x2d, mod, attn, proj, proj, proj, proj, proj, conv_w, w_out)


def kernel(x, c, w_ada, b_ada, w1_gu, w1_down, w_in, q_norm_w, k_norm_w, conv_w, w_out, w2_gu, w2_down):
    bsz, seq, d = x.shape
    depth = w_ada.shape[0]
    width = conv_w.shape[2]
    n_heads = width // HEAD_DIM
    assert w_in.shape[2] == 6 * width and n_heads % ATT_HP == 0 and width == PROJ_TN
    assert seq % max(FFN_TM, PROJ_TM, OUT_TM, ATT_T) == 0 and d % 128 == 0

    x2d = x.reshape(bsz * seq, d)
    for l in range(depth):
        mod = _adaln(c, w_ada[l], b_ada[l]).reshape(bsz, N_MOD, d)
        x2d, (w_in_b, w_out_b) = _ffn(
            x2d, mod, w1_gu[l].astype(BF16), w1_down[l].astype(BF16), shift_row=0, seq=seq,
            cast=(w_in, w_out), layer=l)
        proj, (w2_gu_b, w2_down_b) = _in_proj(
            x2d, mod, w_in_b, q_norm_w[l].reshape(1, HEAD_DIM), k_norm_w[l].reshape(1, HEAD_DIM),
            shift_row=3, seq=seq, cast=(w2_gu, w2_down), layer=l)
        attn = _attention(proj, bsz=bsz, seq=seq, n_heads=n_heads)
        x2d = _out_proj(x2d, mod, attn, proj, conv_w[l], w_out_b, gate_row=5, seq=seq)
        x2d, _ = _ffn(x2d, mod, w2_gu_b, w2_down_b, shift_row=6, seq=seq)
    return x2d.reshape(bsz, seq, d)
```

```python
import functools
import math

import jax
import jax.numpy as jnp
from jax import lax
from jax.experimental import pallas as pl
from jax.experimental.pallas import tpu as pltpu

F32 = jnp.float32
BF16 = jnp.bfloat16

HEAD_DIM = 128
CONV_K = 3
N_MOD = 9
FFN_RES = 0.5
EPS = 1e-6
LOG2E = 1.4426950408889634

F32_EXP2_UNDERFLOW = -150.0

V7X_VMEM_LIMIT_BYTES = 58 * 1024 * 1024
BF16_SUBLANE_TILE = 16

ADA_TN = 1024
FFN_TM = 1024
FFN_TF = 512
FFN_HEAD_TF = 256
PROJ_TM = 1024
PROJ_TN = 1024
ATT_T = 256
ATT_HP = 8
OUT_TM = 512
ROW_CHUNK = 256


def _params(semantics, vmem_bytes):
    return pltpu.CompilerParams(dimension_semantics=semantics,
                                vmem_limit_bytes=min(int(vmem_bytes), V7X_VMEM_LIMIT_BYTES))


def _rms_mod_rows(x_ref, mod_ref, h_ref, shift_row, rows):
    xv = x_ref[rows, :]
    ms = jnp.mean(xv * xv, axis=-1, keepdims=True)
    hn = xv * lax.rsqrt(ms + EPS)
    shift = mod_ref[0, shift_row:shift_row + 1, :]
    scale = mod_ref[0, shift_row + 1:shift_row + 2, :]
    h_ref[rows, :] = (hn * (1.0 + scale) + shift).astype(BF16)


def _adaln_kernel(c_ref, w_ref, b_ref, o_ref):
    cv = c_ref[...]
    c_act = (cv * jax.nn.sigmoid(cv)).astype(BF16)
    o_ref[...] = jnp.dot(c_act, w_ref[...].astype(BF16), preferred_element_type=F32) + b_ref[...]


def _adaln(c, w_ada, b_ada):
    bsz, d = c.shape
    n = w_ada.shape[1]
    return pl.pallas_call(
        _adaln_kernel,
        grid=(n // ADA_TN,),
        in_specs=[pl.BlockSpec((bsz, d), lambda j: (0, 0)),
                  pl.BlockSpec((d, ADA_TN), lambda j: (0, j)),
                  pl.BlockSpec((1, ADA_TN), lambda j: (0, j))],
        out_specs=pl.BlockSpec((bsz, ADA_TN), lambda j: (0, j)),
        out_shape=jax.ShapeDtypeStruct((bsz, n), F32),
        compiler_params=_params(("arbitrary",), 4 * d * ADA_TN * 4),
        name="adaln",
    )(c, w_ada, b_ada.reshape(1, n))


def _cast_blocking(rows, cols, n_i, n_j):
    lane_groups = cols // 128
    n_c = max(k for k in range(1, n_j + 1) if lane_groups % k == 0)
    assert rows % (n_i * BF16_SUBLANE_TILE) == 0 and cols % 128 == 0
    return rows // n_i, cols // n_c, n_c


def _side_cast_specs(cast, layer, n_i, n_j):
    blockings = [_cast_blocking(w.shape[1], w.shape[2], n_i, n_j) for w in cast]

    def cast_map(n_c, lead):
        return lambda i, j: lead + (i, jnp.minimum(j, n_c - 1))

    in_specs = [pl.BlockSpec((None, r, c), cast_map(n_c, (layer,))) for r, c, n_c in blockings]
    out_specs = [pl.BlockSpec((r, c), cast_map(n_c, ())) for r, c, n_c in blockings]
    out_shapes = [jax.ShapeDtypeStruct(w.shape[1:], BF16) for w in cast]
    return in_specs, out_specs, out_shapes, sum(2 * r * c * 6 for r, c, _ in blockings)


def _split_cast_refs(rest, n_cast):
    return rest[:n_cast], rest[n_cast], rest[n_cast + 1:2 * n_cast + 1], rest[2 * n_cast + 1:]


def _side_cast(cast_in, cast_out):
    for src_ref, dst_ref in zip(cast_in, cast_out):
        dst_ref[...] = src_ref[...].astype(BF16)


def _ffn_step(first, last, x_ref, mod_ref, h_ref, o_ref, weights, shift_row):
    if first:
        for r in range(x_ref.shape[0] // ROW_CHUNK):
            _rms_mod_rows(x_ref, mod_ref, h_ref, shift_row, pl.ds(r * ROW_CHUNK, ROW_CHUNK))
    wg, wu, wd = weights()
    h = h_ref[...]
    g = jnp.dot(h, wg, preferred_element_type=F32)
    u = jnp.dot(h, wu, preferred_element_type=F32)
    act = (g * jax.nn.sigmoid(g) * u).astype(BF16)
    part = jnp.dot(act, wd, preferred_element_type=F32)
    if first:
        o_ref[...] = part
    elif last:
        gate = FFN_RES * mod_ref[0, shift_row + 2:shift_row + 3, :]
        o_ref[...] = x_ref[...] + gate * (o_ref[...] + part)
    else:
        o_ref[...] += part


def _ffn_variants(active, f, n_f, step):
    def when(cond):
        return pl.when(cond if active is None else jnp.logical_and(active, cond))
    when(f == 0)(functools.partial(step, True, False))
    when(jnp.logical_and(f > 0, f < n_f - 1))(functools.partial(step, False, False))
    when(f == n_f - 1)(functools.partial(step, False, True))


def _ffn_head_kernel(x_ref, mod_ref, wg32_ref, wu32_ref, wd32_ref, o_ref, wg16_ref, wu16_ref, wd16_ref,
                     h_ref, *, shift_row, n_f):
    def weights():
        blocks = []
        for src_ref, dst_ref in ((wg32_ref, wg16_ref), (wu32_ref, wu16_ref), (wd32_ref, wd16_ref)):
            w = src_ref[...].astype(BF16)
            dst_ref[...] = w
            blocks.append(w)
        return blocks

    def step(first, last):
        _ffn_step(first, last, x_ref, mod_ref, h_ref, o_ref, weights, shift_row)

    _ffn_variants(None, pl.program_id(1), n_f, step)


def _ffn_head(x2d, mod, w_gu, w_down, *, shift_row, layer):
    d = x2d.shape[1]
    d_ff = w_down.shape[1]
    tm, tf = FFN_TM, FFN_HEAD_TF
    n_f = d_ff // tf
    assert n_f >= 3
    once = pl.Buffered(1)
    vmem = 2 * tm * d * 4 + tm * d * 2 + 2 * 3 * d * tf * 6 + 6 * tm * tf * 4 + (4 << 20)
    return pl.pallas_call(
        functools.partial(_ffn_head_kernel, shift_row=shift_row, n_f=n_f),
        grid=(1, n_f),
        in_specs=[pl.BlockSpec((tm, d), lambda i, f: (0, 0), pipeline_mode=once),
                  pl.BlockSpec((1, N_MOD, d), lambda i, f: (0, 0, 0)),
                  pl.BlockSpec((None, d, tf), lambda i, f: (layer, 0, f)),
                  pl.BlockSpec((None, d, tf), lambda i, f: (layer, 0, f + n_f)),
                  pl.BlockSpec((None, tf, d), lambda i, f: (layer, f, 0))],
        out_specs=[pl.BlockSpec((tm, d), lambda i, f: (0, 0)),
                   pl.BlockSpec((d, tf), lambda i, f: (0, f)),
                   pl.BlockSpec((d, tf), lambda i, f: (0, f)),
                   pl.BlockSpec((tf, d), lambda i, f: (f, 0))],
        out_shape=[jax.ShapeDtypeStruct((tm, d), F32),
                   jax.ShapeDtypeStruct((d, d_ff), BF16),
                   jax.ShapeDtypeStruct((d, d_ff), BF16),
                   jax.ShapeDtypeStruct((d_ff, d), BF16)],
        scratch_shapes=[pltpu.VMEM((tm, d), BF16)],
        compiler_params=_params(("arbitrary", "arbitrary"), vmem),
        name="ffn_head",
    )(x2d, mod, w_gu, w_gu, w_down)


def _ffn_kernel(x_ref, mod_ref, wg_ref, wu_ref, wd_ref, *rest, shift_row, n_f, n_cast, has_head):
    head_ref, rest = (rest[0], rest[1:]) if has_head else (None, rest)
    cast_in, o_ref, cast_out, (h_ref,) = _split_cast_refs(rest, n_cast)
    i = pl.program_id(0)
    f = pl.program_id(1)

    def step(first, last):
        _ffn_step(first, last, x_ref, mod_ref, h_ref, o_ref,
                  lambda: (wg_ref[...], wu_ref[...], wd_ref[...]), shift_row)
        _side_cast(cast_in, cast_out)

    _ffn_variants(i > 0 if has_head else None, f, n_f, step)

    if has_head:
        @pl.when(i == 0)
        def _():
            _side_cast(cast_in, cast_out)

        @pl.when(jnp.logical_and(i == 0, f == n_f - 1))
        def _():
            pltpu.sync_copy(head_ref, o_ref)


def _ffn(x2d, mod, w_g, w_u, w_down, *, shift_row, seq, head=None, cast=(), layer=0):
    t, d = x2d.shape
    d_ff = w_down.shape[0]
    n_f = d_ff // FFN_TF
    assert n_f >= 3
    tm = FFN_TM
    n_i = t // tm
    has_head = head is not None
    (wg_arr, g_off), (wu_arr, u_off) = w_g, w_u

    def col(f, i):
        return jnp.where(i == 0, 0, f) if has_head else f

    cast_in_specs, cast_out_specs, cast_shapes, cast_vmem = _side_cast_specs(cast, layer, n_i, n_f)
    vmem = ((2 * tm * d * 4) * 2 + tm * d * 2 + 2 * 3 * d * FFN_TF * 2 + 6 * tm * FFN_TF * 4
            + cast_vmem + (4 << 20))
    outs = pl.pallas_call(
        functools.partial(_ffn_kernel, shift_row=shift_row, n_f=n_f, n_cast=len(cast), has_head=has_head),
        grid=(n_i, n_f),
        in_specs=[pl.BlockSpec((tm, d), lambda i, f: (i, 0)),
                  pl.BlockSpec((1, N_MOD, d), lambda i, f: (i * tm // seq, 0, 0)),
                  pl.BlockSpec((d, FFN_TF), lambda i, f: (0, g_off + col(f, i))),
                  pl.BlockSpec((d, FFN_TF), lambda i, f: (0, u_off + col(f, i))),
                  pl.BlockSpec((FFN_TF, d), lambda i, f: (col(f, i), 0))]
                 + ([pl.BlockSpec(memory_space=pl.ANY)] if has_head else []) + cast_in_specs,
        out_specs=[pl.BlockSpec((tm, d), lambda i, f: (i, 0))] + cast_out_specs,
        out_shape=[jax.ShapeDtypeStruct((t, d), F32)] + cast_shapes,
        scratch_shapes=[pltpu.VMEM((tm, d), BF16)],
        compiler_params=_params(("arbitrary", "arbitrary"), vmem),
        name="ffn",
    )(x2d, mod, wg_arr, wu_arr, w_down, *((head,) if has_head else ()), *cast)
    return outs[0], outs[1:]


def _head_rms_gain(p, gain):
    outs = []
    for hd in range(p.shape[1] // HEAD_DIM):
        ph = p[:, hd * HEAD_DIM:(hd + 1) * HEAD_DIM]
        ms = jnp.mean(ph * ph, axis=-1, keepdims=True)
        outs.append(ph * lax.rsqrt(ms + EPS) * gain)
    return jnp.concatenate(outs, axis=1)


def _in_proj_kernel(x_ref, mod_ref, w_ref, qg_ref, kg_ref, *rest, shift_row, q_scale, n_cast):
    cast_in, o_ref, cast_out, (h_ref,) = _split_cast_refs(rest, n_cast)
    n = pl.program_id(1)
    tm = x_ref.shape[0]

    @pl.when(n == 0)
    def _():
        for r in range(tm // ROW_CHUNK):
            _rms_mod_rows(x_ref, mod_ref, h_ref, shift_row, pl.ds(r * ROW_CHUNK, ROW_CHUNK))
        p = jnp.dot(h_ref[...], w_ref[...], preferred_element_type=F32)
        o_ref[...] = _head_rms_gain(p, qg_ref[...] * q_scale).astype(BF16)
        _side_cast(cast_in, cast_out)

    @pl.when(n == 1)
    def _():
        p = jnp.dot(h_ref[...], w_ref[...], preferred_element_type=F32)
        o_ref[...] = _head_rms_gain(p, kg_ref[...]).astype(BF16)
        _side_cast(cast_in, cast_out)

    @pl.when(n >= 2)
    def _():
        o_ref[...] = jnp.dot(h_ref[...], w_ref[...], preferred_element_type=F32).astype(BF16)
        _side_cast(cast_in, cast_out)


def _in_proj(x2d, mod, w_in, q_gain, k_gain, *, shift_row, seq, cast=(), layer=0):
    t, d = x2d.shape
    n_cols = w_in.shape[1]
    tm = PROJ_TM
    n_i, n_n = t // tm, n_cols // PROJ_TN
    cast_in_specs, cast_out_specs, cast_shapes, cast_vmem = _side_cast_specs(cast, layer, n_i, n_n)
    vmem = (2 * tm * d * 4 + tm * d * 2 + 2 * d * PROJ_TN * 2 + 2 * tm * PROJ_TN * 2 + 6 * tm * PROJ_TN * 4
            + cast_vmem + (4 << 20))
    q_scale = LOG2E / math.sqrt(HEAD_DIM)
    outs = pl.pallas_call(
        functools.partial(_in_proj_kernel, shift_row=shift_row, q_scale=q_scale, n_cast=len(cast)),
        grid=(n_i, n_n),
        in_specs=[pl.BlockSpec((tm, d), lambda i, n: (i, 0)),
                  pl.BlockSpec((1, N_MOD, d), lambda i, n: (i * tm // seq, 0, 0)),
                  pl.BlockSpec((d, PROJ_TN), lambda i, n: (0, n)),
                  pl.BlockSpec((1, HEAD_DIM), lambda i, n: (0, 0)),
                  pl.BlockSpec((1, HEAD_DIM), lambda i, n: (0, 0))] + cast_in_specs,
        out_specs=[pl.BlockSpec((tm, PROJ_TN), lambda i, n: (i, n))] + cast_out_specs,
        out_shape=[jax.ShapeDtypeStruct((t, n_cols), BF16)] + cast_shapes,
        scratch_shapes=[pltpu.VMEM((tm, d), BF16)],
        compiler_params=_params(("arbitrary", "arbitrary"), vmem),
        name="in_proj",
    )(x2d, mod, w_in, q_gain, k_gain, *cast)
    return outs[0], outs[1:]


def _attn_step(qs, ks, vs, neg_tri, carries, causal_mask):
    n_h = len(qs)
    z, sp16, rs, neg_incl, a, pv = ({} for _ in range(6))

    def scores(h):
        z[h] = lax.dot_general(qs[h], ks[h], (((1,), (1,)), ((), ())), preferred_element_type=F32)

    def softplus(h):
        sp = jnp.maximum(z[h], 0.0) + jnp.log(1.0 + jnp.exp2(jnp.minimum(z[h], -z[h]))) * LOG2E
        if causal_mask is not None:
            sp = jnp.where(causal_mask, sp, 0.0)
        sp16[h] = sp.astype(BF16)
        rs[h] = jnp.sum(sp, axis=-1, keepdims=True)

    def suffix_sum(h):
        neg_incl[h] = jnp.dot(sp16[h], neg_tri, preferred_element_type=F32)

    def weights(h):
        w = jnp.exp2(z[h] + neg_incl[h] + carries[h])
        if causal_mask is not None:
            w = jnp.where(causal_mask, w, 0.0)
        a[h] = w.astype(BF16)

    def values(h):
        pv[h] = jnp.dot(a[h], vs[h], preferred_element_type=F32)

    stages = (scores, softplus, suffix_sum, weights, values)
    for wave in range(n_h + len(stages) - 1):
        for s in reversed(range(len(stages))):
            h = wave - s
            if 0 <= h < n_h:
                stages[s](h)
    return [pv[h] for h in range(n_h)], [rs[h] for h in range(n_h)]


def _attention_kernel(q_ref, k_ref, v_ref, o_ref):
    i = pl.program_id(2)
    t = q_ref.shape[0]
    n_h = q_ref.shape[1] // HEAD_DIM
    row = lax.broadcasted_iota(jnp.int32, (t, t), 0)
    col = lax.broadcasted_iota(jnp.int32, (t, t), 1)
    neg_tri = jnp.where(row >= col, -1.0, 0.0).astype(BF16)
    causal = col < row

    def lanes(hd):
        return slice(hd * HEAD_DIM, (hd + 1) * HEAD_DIM)

    qs = [q_ref[:, lanes(hd)] for hd in range(n_h)]

    def step(blk, accs, carries, mask):
        ks = [k_ref[blk, lanes(hd)] for hd in range(n_h)]
        vs = [v_ref[blk, lanes(hd)] for hd in range(n_h)]
        pvs, rss = _attn_step(qs, ks, vs, neg_tri, carries, mask)
        accs = tuple(pvs) if accs is None else tuple(acc + pv for acc, pv in zip(accs, pvs))
        carries = tuple(cr - rs for cr, rs in zip(carries, rss))
        live = carries[0]
        for cr in carries[1:]:
            live = jnp.maximum(live, cr)
        return accs, carries, jnp.max(live)

    zero = jnp.zeros((t, 1), F32)
    accs, carries, live = step(pl.ds(pl.multiple_of(i * t, t), t), None, (zero,) * n_h, causal)

    def cond(state):
        jj, live = state[0], state[1]
        return jnp.logical_and(jj < i, live > F32_EXP2_UNDERFLOW)

    def body(state):
        jj, _, accs, carries = state
        blk = pl.ds(pl.multiple_of((i - 1 - jj) * t, t), t)
        accs, carries, live = step(blk, accs, carries, None)
        return jj + 1, live, accs, carries

    _, _, accs, _ = lax.while_loop(cond, body, (jnp.int32(0), live, accs, carries))
    for hd in range(n_h):
        o_ref[:, lanes(hd)] = accs[hd].astype(o_ref.dtype)


def _attention(proj, *, bsz, seq, n_heads):
    t = proj.shape[0]
    nq = seq // ATT_T
    group = ATT_HP * HEAD_DIM
    n_groups = n_heads // ATT_HP
    vmem = 2 * 2 * seq * group * 2 + ATT_HP * 16 * ATT_T * ATT_T * 4 + (4 << 20)
    return pl.pallas_call(
        _attention_kernel,
        grid=(bsz, n_groups, nq),
        in_specs=[pl.BlockSpec((ATT_T, group), lambda b, g, i: (b * nq + i, g)),
                  pl.BlockSpec((seq, group), lambda b, g, i: (b, n_groups + g)),
                  pl.BlockSpec((seq, group), lambda b, g, i: (b, 2 * n_groups + g))],
        out_specs=pl.BlockSpec((ATT_T, group), lambda b, g, i: (b * nq + i, g)),
        out_shape=jax.ShapeDtypeStruct((t, n_heads * HEAD_DIM), BF16),
        compiler_params=_params(("parallel", "parallel", "arbitrary"), vmem),
        name="attention",
    )(proj, proj, proj)


def _out_proj_kernel(x_ref, mod_ref, attn_ref, gb_ref, gc_ref, u_ref, gch_ref, uh_ref, cw_ref, w_ref,
                     o_ref, cu_ref, *, gate_row, seq):
    i = pl.program_id(0)
    tm = x_ref.shape[0]
    halo = gch_ref.shape[0]
    width = attn_ref.shape[1]

    mix = jnp.dot(attn_ref[...], w_ref[0:width, :], preferred_element_type=F32)

    prev = gch_ref[...].astype(F32) * uh_ref[...].astype(F32)
    seq_start = (i * tm) % seq == 0
    cu_ref[0:halo, :] = jnp.where(seq_start, 0.0, prev)
    cu_ref[halo:halo + tm, :] = gc_ref[...].astype(F32) * u_ref[...].astype(F32)

    conv = cu_ref[halo:halo + tm, :] * cw_ref[CONV_K - 1:CONV_K, :]
    for k in range(CONV_K - 1):
        back = CONV_K - 1 - k
        conv = conv + cu_ref[halo - back:halo - back + tm, :] * cw_ref[k:k + 1, :]
    y = (gb_ref[...].astype(F32) * conv).astype(BF16)

    mix = mix + jnp.dot(y, w_ref[width:, :], preferred_element_type=F32)
    gate = mod_ref[0, gate_row:gate_row + 1, :]
    o_ref[...] = x_ref[...] + gate * mix


def _out_proj(x2d, mod, attn, proj, conv_w, w_out, *, gate_row, seq):
    t, d = x2d.shape
    width = attn.shape[1]
    tm = OUT_TM
    halo = BF16_SUBLANE_TILE
    per = tm // halo
    b_col, c_col, u_col = 3, 4, 5

    def halo_map(col):
        return lambda i: (jnp.maximum(i * per - 1, 0), col)

    vmem = (2 * 2 * tm * d * 4 + 2 * 4 * tm * width * 2 + 2 * d * d * 2 + (tm + halo) * width * 4
            + 8 * tm * width * 4 + (4 << 20))
    return pl.pallas_call(
        functools.partial(_out_proj_kernel, gate_row=gate_row, seq=seq),
        grid=(t // tm,),
        in_specs=[pl.BlockSpec((tm, d), lambda i: (i, 0)),
                  pl.BlockSpec((1, N_MOD, d), lambda i: (i * tm // seq, 0, 0)),
                  pl.BlockSpec((tm, width), lambda i: (i, 0)),
                  pl.BlockSpec((tm, width), lambda i: (i, b_col)),
                  pl.BlockSpec((tm, width), lambda i: (i, c_col)),
                  pl.BlockSpec((tm, width), lambda i: (i, u_col)),
                  pl.BlockSpec((halo, width), halo_map(c_col)),
                  pl.BlockSpec((halo, width), halo_map(u_col)),
                  pl.BlockSpec((CONV_K, width), lambda i: (0, 0)),
                  pl.BlockSpec((d, d), lambda i: (0, 0))],
        out_specs=pl.BlockSpec((tm, d), lambda i: (i, 0)),
        out_shape=jax.ShapeDtypeStruct((t, d), F32),
        scratch_shapes=[pltpu.VMEM((tm + halo, width), F32)],
        compiler_params=_params(("parallel",), vmem),
        name="out_proj",
    )(x2d, mod, attn, proj, proj, proj, proj, proj, conv_w, w_out)


def kernel(x, c, w_ada, b_ada, w1_gu, w1_down, w_in, q_norm_w, k_norm_w, conv_w, w_out, w2_gu, w2_down):
    bsz, seq, d = x.shape
    depth = w_ada.shape[0]
    width = conv_w.shape[2]
    n_heads = width // HEAD_DIM
    assert w_in.shape[2] == 6 * width and n_heads % ATT_HP == 0 and width == PROJ_TN
    assert seq % max(FFN_TM, PROJ_TM, OUT_TM, ATT_T) == 0 and d % 128 == 0

    x2d = x.reshape(bsz * seq, d)
    for l in range(depth):
        mod = _adaln(c, w_ada[l], b_ada[l]).reshape(bsz, N_MOD, d)
        head, w1_g_b, w1_u_b, w1_down_b = _ffn_head(x2d, mod, w1_gu, w1_down, shift_row=0, layer=l)
        x2d, (w_in_b, w_out_b) = _ffn(
            x2d, mod, (w1_g_b, 0), (w1_u_b, 0), w1_down_b, shift_row=0, seq=seq, head=head,
            cast=(w_in, w_out), layer=l)
        proj, (w2_gu_b, w2_down_b) = _in_proj(
            x2d, mod, w_in_b, q_norm_w[l].reshape(1, HEAD_DIM), k_norm_w[l].reshape(1, HEAD_DIM),
            shift_row=3, seq=seq, cast=(w2_gu, w2_down), layer=l)
        attn = _attention(proj, bsz=bsz, seq=seq, n_heads=n_heads)
        x2d = _out_proj(x2d, mod, attn, proj, conv_w[l], w_out_b, gate_row=5, seq=seq)
        n_f = w2_down_b.shape[0] // FFN_TF
        x2d, _ = _ffn(x2d, mod, (w2_gu_b, 0), (w2_gu_b, n_f), w2_down_b, shift_row=6, seq=seq)
    return x2d.reshape(bsz, seq, d)
```

```python
import functools
import math

import jax
import jax.numpy as jnp
from jax import lax
from jax.experimental import pallas as pl
from jax.experimental.pallas import tpu as pltpu

F32 = jnp.float32
BF16 = jnp.bfloat16

HEAD_DIM = 128
CONV_K = 3
N_MOD = 9
FFN_RES = 0.5
EPS = 1e-6
LOG2E = 1.4426950408889634

F32_EXP2_UNDERFLOW = -150.0
NO_BLOCK_CARRY = -1e30

V7X_VMEM_LIMIT_BYTES = 58 * 1024 * 1024
BF16_SUBLANE_TILE = 16

ADA_TN = 1024
FFN_TM = 1024
FFN_TF = 512
FFN_HEAD_TF = 256
PROJ_TM = 1024
PROJ_TN = 1024
ATT_T = 256
OUT_TN = 256
ROW_CHUNK = 256


def _params(semantics, vmem_bytes):
    return pltpu.CompilerParams(dimension_semantics=semantics,
                                vmem_limit_bytes=min(int(vmem_bytes), V7X_VMEM_LIMIT_BYTES))


def _rms_mod_rows(x_ref, mod_ref, h_ref, shift_row, rows):
    xv = x_ref[rows, :]
    ms = jnp.mean(xv * xv, axis=-1, keepdims=True)
    hn = xv * lax.rsqrt(ms + EPS)
    shift = mod_ref[0, shift_row:shift_row + 1, :]
    scale = mod_ref[0, shift_row + 1:shift_row + 2, :]
    h_ref[rows, :] = (hn * (1.0 + scale) + shift).astype(BF16)


def _adaln_kernel(c_ref, w_ref, b_ref, o_ref):
    cv = c_ref[...]
    c_act = (cv * jax.nn.sigmoid(cv)).astype(BF16)
    o_ref[...] = jnp.dot(c_act, w_ref[...].astype(BF16), preferred_element_type=F32) + b_ref[...]


def _adaln(c, w_ada, b_ada):
    bsz, d = c.shape
    n = w_ada.shape[1]
    return pl.pallas_call(
        _adaln_kernel,
        grid=(n // ADA_TN,),
        in_specs=[pl.BlockSpec((bsz, d), lambda j: (0, 0)),
                  pl.BlockSpec((d, ADA_TN), lambda j: (0, j)),
                  pl.BlockSpec((1, ADA_TN), lambda j: (0, j))],
        out_specs=pl.BlockSpec((bsz, ADA_TN), lambda j: (0, j)),
        out_shape=jax.ShapeDtypeStruct((bsz, n), F32),
        compiler_params=_params(("arbitrary",), 4 * d * ADA_TN * 4),
        name="adaln",
    )(c, w_ada, b_ada.reshape(1, n))


def _cast_blocking(rows, cols, n_i, n_j):
    lane_groups = cols // 128
    n_c = max(k for k in range(1, n_j + 1) if lane_groups % k == 0)
    assert rows % (n_i * BF16_SUBLANE_TILE) == 0 and cols % 128 == 0
    return rows // n_i, cols // n_c, n_c


def _side_cast_specs(cast, layer, n_i, n_j):
    blockings = [_cast_blocking(w.shape[1], w.shape[2], n_i, n_j) for w in cast]

    def cast_map(n_c, lead):
        return lambda i, j: lead + (i, jnp.minimum(j, n_c - 1))

    in_specs = [pl.BlockSpec((None, r, c), cast_map(n_c, (layer,))) for r, c, n_c in blockings]
    out_specs = [pl.BlockSpec((r, c), cast_map(n_c, ())) for r, c, n_c in blockings]
    out_shapes = [jax.ShapeDtypeStruct(w.shape[1:], BF16) for w in cast]
    return in_specs, out_specs, out_shapes, sum(2 * r * c * 6 for r, c, _ in blockings)


def _split_cast_refs(rest, n_cast):
    return rest[:n_cast], rest[n_cast], rest[n_cast + 1:2 * n_cast + 1], rest[2 * n_cast + 1:]


def _side_cast(cast_in, cast_out):
    for src_ref, dst_ref in zip(cast_in, cast_out):
        dst_ref[...] = src_ref[...].astype(BF16)


def _ffn_step(first, last, x_ref, mod_ref, h_ref, o_ref, weights, shift_row):
    if first:
        for r in range(x_ref.shape[0] // ROW_CHUNK):
            _rms_mod_rows(x_ref, mod_ref, h_ref, shift_row, pl.ds(r * ROW_CHUNK, ROW_CHUNK))
    wg, wu, wd = weights()
    h = h_ref[...]
    g = jnp.dot(h, wg, preferred_element_type=F32)
    u = jnp.dot(h, wu, preferred_element_type=F32)
    act = (g * jax.nn.sigmoid(g) * u).astype(BF16)
    part = jnp.dot(act, wd, preferred_element_type=F32)
    if first:
        o_ref[...] = part
    elif last:
        gate = FFN_RES * mod_ref[0, shift_row + 2:shift_row + 3, :]
        o_ref[...] = x_ref[...] + gate * (o_ref[...] + part)
    else:
        o_ref[...] += part


def _ffn_variants(active, f, n_f, step):
    def when(cond):
        return pl.when(cond if active is None else jnp.logical_and(active, cond))
    when(f == 0)(functools.partial(step, True, False))
    when(jnp.logical_and(f > 0, f < n_f - 1))(functools.partial(step, False, False))
    when(f == n_f - 1)(functools.partial(step, False, True))


def _ffn_head_kernel(x_ref, mod_ref, wg32_ref, wu32_ref, wd32_ref, o_ref, wg16_ref, wu16_ref, wd16_ref,
                     h_ref, *, shift_row, n_f):
    def weights():
        blocks = []
        for src_ref, dst_ref in ((wg32_ref, wg16_ref), (wu32_ref, wu16_ref), (wd32_ref, wd16_ref)):
            w = src_ref[...].astype(BF16)
            dst_ref[...] = w
            blocks.append(w)
        return blocks

    def step(first, last):
        _ffn_step(first, last, x_ref, mod_ref, h_ref, o_ref, weights, shift_row)

    _ffn_variants(None, pl.program_id(1), n_f, step)


def _ffn_head(x2d, mod, w_gu, w_down, *, shift_row, layer):
    d = x2d.shape[1]
    d_ff = w_down.shape[1]
    tm, tf = FFN_TM, FFN_HEAD_TF
    n_f = d_ff // tf
    assert n_f >= 3
    once = pl.Buffered(1)
    vmem = 2 * tm * d * 4 + tm * d * 2 + 2 * 3 * d * tf * 6 + 6 * tm * tf * 4 + (4 << 20)
    return pl.pallas_call(
        functools.partial(_ffn_head_kernel, shift_row=shift_row, n_f=n_f),
        grid=(1, n_f),
        in_specs=[pl.BlockSpec((tm, d), lambda i, f: (0, 0), pipeline_mode=once),
                  pl.BlockSpec((1, N_MOD, d), lambda i, f: (0, 0, 0)),
                  pl.BlockSpec((None, d, tf), lambda i, f: (layer, 0, f)),
                  pl.BlockSpec((None, d, tf), lambda i, f: (layer, 0, f + n_f)),
                  pl.BlockSpec((None, tf, d), lambda i, f: (layer, f, 0))],
        out_specs=[pl.BlockSpec((tm, d), lambda i, f: (0, 0)),
                   pl.BlockSpec((d, tf), lambda i, f: (0, f)),
                   pl.BlockSpec((d, tf), lambda i, f: (0, f)),
                   pl.BlockSpec((tf, d), lambda i, f: (f, 0))],
        out_shape=[jax.ShapeDtypeStruct((tm, d), F32),
                   jax.ShapeDtypeStruct((d, d_ff), BF16),
                   jax.ShapeDtypeStruct((d, d_ff), BF16),
                   jax.ShapeDtypeStruct((d_ff, d), BF16)],
        scratch_shapes=[pltpu.VMEM((tm, d), BF16)],
        compiler_params=_params(("arbitrary", "arbitrary"), vmem),
        name="ffn_head",
    )(x2d, mod, w_gu, w_gu, w_down)


def _ffn_kernel(x_ref, mod_ref, wg_ref, wu_ref, wd_ref, *rest, shift_row, n_f, n_cast, has_head):
    head_ref, rest = (rest[0], rest[1:]) if has_head else (None, rest)
    cast_in, o_ref, cast_out, (h_ref,) = _split_cast_refs(rest, n_cast)
    i = pl.program_id(0)
    f = pl.program_id(1)

    def step(first, last):
        _ffn_step(first, last, x_ref, mod_ref, h_ref, o_ref,
                  lambda: (wg_ref[...], wu_ref[...], wd_ref[...]), shift_row)
        _side_cast(cast_in, cast_out)

    _ffn_variants(i > 0 if has_head else None, f, n_f, step)

    if has_head:
        @pl.when(i == 0)
        def _():
            _side_cast(cast_in, cast_out)

        @pl.when(jnp.logical_and(i == 0, f == n_f - 1))
        def _():
            pltpu.sync_copy(head_ref, o_ref)


def _ffn(x2d, mod, w_g, w_u, w_down, *, shift_row, seq, head=None, cast=(), layer=0):
    t, d = x2d.shape
    d_ff = w_down.shape[0]
    n_f = d_ff // FFN_TF
    assert n_f >= 3
    tm = FFN_TM
    n_i = t // tm
    has_head = head is not None
    (wg_arr, g_off), (wu_arr, u_off) = w_g, w_u

    def col(f, i):
        return jnp.where(i == 0, 0, f) if has_head else f

    cast_in_specs, cast_out_specs, cast_shapes, cast_vmem = _side_cast_specs(cast, layer, n_i, n_f)
    vmem = ((2 * tm * d * 4) * 2 + tm * d * 2 + 2 * 3 * d * FFN_TF * 2 + 6 * tm * FFN_TF * 4
            + cast_vmem + (4 << 20))
    outs = pl.pallas_call(
        functools.partial(_ffn_kernel, shift_row=shift_row, n_f=n_f, n_cast=len(cast), has_head=has_head),
        grid=(n_i, n_f),
        in_specs=[pl.BlockSpec((tm, d), lambda i, f: (i, 0)),
                  pl.BlockSpec((1, N_MOD, d), lambda i, f: (i * tm // seq, 0, 0)),
                  pl.BlockSpec((d, FFN_TF), lambda i, f: (0, g_off + col(f, i))),
                  pl.BlockSpec((d, FFN_TF), lambda i, f: (0, u_off + col(f, i))),
                  pl.BlockSpec((FFN_TF, d), lambda i, f: (col(f, i), 0))]
                 + ([pl.BlockSpec(memory_space=pl.ANY)] if has_head else []) + cast_in_specs,
        out_specs=[pl.BlockSpec((tm, d), lambda i, f: (i, 0))] + cast_out_specs,
        out_shape=[jax.ShapeDtypeStruct((t, d), F32)] + cast_shapes,
        scratch_shapes=[pltpu.VMEM((tm, d), BF16)],
        compiler_params=_params(("arbitrary", "arbitrary"), vmem),
        name="ffn",
    )(x2d, mod, wg_arr, wu_arr, w_down, *((head,) if has_head else ()), *cast)
    return outs[0], outs[1:]


def _head_rms_gain(p, gain):
    outs = []
    for hd in range(p.shape[1] // HEAD_DIM):
        ph = p[:, hd * HEAD_DIM:(hd + 1) * HEAD_DIM]
        ms = jnp.mean(ph * ph, axis=-1, keepdims=True)
        outs.append(ph * lax.rsqrt(ms + EPS) * gain)
    return jnp.concatenate(outs, axis=1)


def _in_proj_kernel(x_ref, mod_ref, w_ref, qg_ref, kg_ref, *rest, shift_row, q_scale, n_cast):
    cast_in, o_ref, cast_out, (h_ref,) = _split_cast_refs(rest, n_cast)
    n = pl.program_id(1)
    tm = x_ref.shape[0]

    @pl.when(n == 0)
    def _():
        for r in range(tm // ROW_CHUNK):
            _rms_mod_rows(x_ref, mod_ref, h_ref, shift_row, pl.ds(r * ROW_CHUNK, ROW_CHUNK))
        p = jnp.dot(h_ref[...], w_ref[...], preferred_element_type=F32)
        o_ref[...] = _head_rms_gain(p, qg_ref[...] * q_scale).astype(BF16)
        _side_cast(cast_in, cast_out)

    @pl.when(n == 1)
    def _():
        p = jnp.dot(h_ref[...], w_ref[...], preferred_element_type=F32)
        o_ref[...] = _head_rms_gain(p, kg_ref[...]).astype(BF16)
        _side_cast(cast_in, cast_out)

    @pl.when(n >= 2)
    def _():
        o_ref[...] = jnp.dot(h_ref[...], w_ref[...], preferred_element_type=F32).astype(BF16)
        _side_cast(cast_in, cast_out)


def _in_proj(x2d, mod, w_in, q_gain, k_gain, *, shift_row, seq, cast=(), layer=0):
    t, d = x2d.shape
    n_cols = w_in.shape[1]
    tm = PROJ_TM
    n_i, n_n = t // tm, n_cols // PROJ_TN
    cast_in_specs, cast_out_specs, cast_shapes, cast_vmem = _side_cast_specs(cast, layer, n_i, n_n)
    vmem = (2 * tm * d * 4 + tm * d * 2 + 2 * d * PROJ_TN * 2 + 2 * tm * PROJ_TN * 2 + 6 * tm * PROJ_TN * 4
            + cast_vmem + (4 << 20))
    q_scale = LOG2E / math.sqrt(HEAD_DIM)
    outs = pl.pallas_call(
        functools.partial(_in_proj_kernel, shift_row=shift_row, q_scale=q_scale, n_cast=len(cast)),
        grid=(n_i, n_n),
        in_specs=[pl.BlockSpec((tm, d), lambda i, n: (i, 0)),
                  pl.BlockSpec((1, N_MOD, d), lambda i, n: (i * tm // seq, 0, 0)),
                  pl.BlockSpec((d, PROJ_TN), lambda i, n: (0, n)),
                  pl.BlockSpec((1, HEAD_DIM), lambda i, n: (0, 0)),
                  pl.BlockSpec((1, HEAD_DIM), lambda i, n: (0, 0))] + cast_in_specs,
        out_specs=[pl.BlockSpec((tm, PROJ_TN), lambda i, n: (i, n))] + cast_out_specs,
        out_shape=[jax.ShapeDtypeStruct((t, n_cols), BF16)] + cast_shapes,
        scratch_shapes=[pltpu.VMEM((tm, d), BF16)],
        compiler_params=_params(("arbitrary", "arbitrary"), vmem),
        name="in_proj",
    )(x2d, mod, w_in, q_gain, k_gain, *cast)
    return outs[0], outs[1:]


def _attn_step(chains, neg_tri, fillers=()):
    n_c = len(chains)
    z, sp16, rs, neg_incl, a, pv = ({} for _ in range(6))

    def scores(c):
        q, k = chains[c][0], chains[c][1]
        z[c] = lax.dot_general(q, k, (((1,), (1,)), ((), ())), preferred_element_type=F32)

    def softplus(c):
        mask = chains[c][3]
        sp = jnp.maximum(z[c], 0.0) + jnp.log(1.0 + jnp.exp2(jnp.minimum(z[c], -z[c]))) * LOG2E
        if mask is not None:
            sp = jnp.where(mask, sp, 0.0)
        sp16[c] = sp.astype(BF16)
        rs[c] = jnp.sum(sp, axis=-1, keepdims=True)

    def suffix_sum(c):
        neg_incl[c] = jnp.dot(sp16[c], neg_tri, preferred_element_type=F32)

    def weights(c):
        mask, carry_fn = chains[c][3], chains[c][4]
        w = jnp.exp2(z[c] + neg_incl[c] + carry_fn(rs))
        if mask is not None:
            w = jnp.where(mask, w, 0.0)
        a[c] = w.astype(BF16)

    def values(c):
        pv[c] = jnp.dot(a[c], chains[c][2], preferred_element_type=F32)

    stages = (scores, softplus, suffix_sum, weights, values)
    n_waves = n_c + len(stages) - 1
    done_fill = 0
    for wave in range(n_waves):
        for s in range(len(stages)):
            c = wave - s
            if 0 <= c < n_c:
                stages[s](c)
        while done_fill < (wave + 1) * len(fillers) // n_waves:
            fillers[done_fill]()
            done_fill += 1
    return [pv[c] for c in range(n_c)], [rs[c] for c in range(n_c)]


def _mixer_kernel(q_ref, k_ref, v_ref, x_ref, mod_ref, gb_ref, gc_ref, u_ref, gch_ref, uh_ref, cw_ref, w_ref,
                  o_ref, attn_ref, cu_ref, *, gate_row, nq):
    i = pl.program_id(1)
    t = q_ref.shape[0]
    n_h = q_ref.shape[1] // HEAD_DIM
    halo = gch_ref.shape[0]
    width = attn_ref.shape[1]
    d = o_ref.shape[1]

    @pl.when(i == 0)
    def _():
        attn_ref[...] = jnp.zeros_like(attn_ref)

    attn_prev = attn_ref[...]
    prev = gch_ref[...].astype(F32) * uh_ref[...].astype(F32)
    cu_ref[0:halo, :] = jnp.where(i <= 1, 0.0, prev)
    cu_ref[halo:halo + t, :] = gc_ref[...].astype(F32) * u_ref[...].astype(F32)
    conv = cu_ref[halo:halo + t, :] * cw_ref[CONV_K - 1:CONV_K, :]
    for k in range(CONV_K - 1):
        back = CONV_K - 1 - k
        conv = conv + cu_ref[halo - back:halo - back + t, :] * cw_ref[k:k + 1, :]
    y = (gb_ref[...].astype(F32) * conv).astype(BF16)
    gate = mod_ref[0, gate_row:gate_row + 1, :]

    def out_chunk(c):
        cols = slice(c * OUT_TN, (c + 1) * OUT_TN)
        mix = jnp.dot(attn_prev, w_ref[0:width, cols], preferred_element_type=F32)
        mix = mix + jnp.dot(y, w_ref[width:, cols], preferred_element_type=F32)
        o_ref[:, cols] = x_ref[:, cols] + gate[:, cols] * mix

    fillers = [functools.partial(out_chunk, c) for c in range(d // OUT_TN)]

    qi = jnp.minimum(i, nq - 1)
    row = lax.broadcasted_iota(jnp.int32, (t, t), 0)
    col = lax.broadcasted_iota(jnp.int32, (t, t), 1)
    neg_tri = jnp.where(row >= col, -1.0, 0.0).astype(BF16)
    causal = col < row

    def lanes(hd):
        return slice(hd * HEAD_DIM, (hd + 1) * HEAD_DIM)

    def key_block(j):
        blk = pl.ds(pl.multiple_of(j * t, t), t)
        return [(k_ref[blk, lanes(hd)], v_ref[blk, lanes(hd)]) for hd in range(n_h)]

    qs = [q_ref[:, lanes(hd)] for hd in range(n_h)]
    zero = jnp.zeros((t, 1), F32)
    base = jnp.where(qi == 0, NO_BLOCK_CARRY, 0.0)
    kv_d, kv_p = key_block(qi), key_block(jnp.maximum(qi - 1, 0))
    chains = [(qs[hd], kv_d[hd][0], kv_d[hd][1], causal, lambda rs: zero) for hd in range(n_h)]
    chains += [(qs[hd], kv_p[hd][0], kv_p[hd][1], None, lambda rs, hd=hd: base - rs[hd])
               for hd in range(n_h)]
    pvs, rss = _attn_step(chains, neg_tri, fillers)
    accs = tuple(pvs[hd] + pvs[n_h + hd] for hd in range(n_h))
    carries = tuple(-rss[hd] - rss[n_h + hd] for hd in range(n_h))

    def live_max(carries):
        live = carries[0]
        for cr in carries[1:]:
            live = jnp.maximum(live, cr)
        return jnp.max(live)

    def cond(state):
        jj, live = state[0], state[1]
        return jnp.logical_and(jj < qi, live > F32_EXP2_UNDERFLOW)

    def body(state):
        jj, _, accs, carries = state
        kv = key_block(qi - 1 - jj)
        pvs, rss = _attn_step([(qs[hd], kv[hd][0], kv[hd][1], None, lambda rs, hd=hd: carries[hd])
                               for hd in range(n_h)], neg_tri)
        accs = tuple(acc + pv for acc, pv in zip(accs, pvs))
        carries = tuple(cr - rs for cr, rs in zip(carries, rss))
        return jj + 1, live_max(carries), accs, carries

    _, _, accs, _ = lax.while_loop(cond, body, (jnp.int32(1), live_max(carries), accs, carries))
    for hd in range(n_h):
        attn_ref[:, lanes(hd)] = accs[hd].astype(attn_ref.dtype)


def _mixer(x2d, mod, proj, conv_w, w_out, *, gate_row, bsz, seq, n_heads):
    t_all, d = x2d.shape
    width = n_heads * HEAD_DIM
    t = ATT_T
    nq = seq // t
    halo = BF16_SUBLANE_TILE
    per = t // halo
    k_col, v_col, b_col, c_col, u_col = 1, 2, 3, 4, 5
    once = pl.Buffered(1)

    def out_row(b, i):
        return b * nq + jnp.maximum(i - 1, 0)

    def halo_map(col):
        return lambda b, i: (jnp.maximum(out_row(b, i) * per - 1, 0), col)

    vmem = (2 * seq * width * 2 + d * d * 2 + 2 * 2 * t * d * 4 + 2 * 4 * t * width * 2
            + (t + halo) * width * 4 + 2 * n_heads * 12 * t * t * 4 + (4 << 20))
    return pl.pallas_call(
        functools.partial(_mixer_kernel, gate_row=gate_row, nq=nq),
        grid=(bsz, nq + 1),
        in_specs=[pl.BlockSpec((t, width), lambda b, i: (b * nq + jnp.minimum(i, nq - 1), 0)),
                  pl.BlockSpec((seq, width), lambda b, i: (b, k_col), pipeline_mode=once),
                  pl.BlockSpec((seq, width), lambda b, i: (b, v_col), pipeline_mode=once),
                  pl.BlockSpec((t, d), lambda b, i: (out_row(b, i), 0)),
                  pl.BlockSpec((1, N_MOD, d), lambda b, i: (b, 0, 0)),
                  pl.BlockSpec((t, width), lambda b, i: (out_row(b, i), b_col)),
                  pl.BlockSpec((t, width), lambda b, i: (out_row(b, i), c_col)),
                  pl.BlockSpec((t, width), lambda b, i: (out_row(b, i), u_col)),
                  pl.BlockSpec((halo, width), halo_map(c_col)),
                  pl.BlockSpec((halo, width), halo_map(u_col)),
                  pl.BlockSpec((CONV_K, width), lambda b, i: (0, 0)),
                  pl.BlockSpec((d, d), lambda b, i: (0, 0))],
        out_specs=pl.BlockSpec((t, d), lambda b, i: (out_row(b, i), 0)),
        out_shape=jax.ShapeDtypeStruct((t_all, d), F32),
        scratch_shapes=[pltpu.VMEM((t, width), BF16), pltpu.VMEM((t + halo, width), F32)],
        compiler_params=_params(("arbitrary", "arbitrary"), vmem),
        name="mixer",
    )(proj, proj, proj, x2d, mod, proj, proj, proj, proj, proj, conv_w, w_out)


def kernel(x, c, w_ada, b_ada, w1_gu, w1_down, w_in, q_norm_w, k_norm_w, conv_w, w_out, w2_gu, w2_down):
    bsz, seq, d = x.shape
    depth = w_ada.shape[0]
    width = conv_w.shape[2]
    n_heads = width // HEAD_DIM
    assert w_in.shape[2] == 6 * width and width == PROJ_TN
    assert seq % max(FFN_TM, PROJ_TM, ATT_T) == 0 and d % OUT_TN == 0

    x2d = x.reshape(bsz * seq, d)
    for l in range(depth):
        mod = _adaln(c, w_ada[l], b_ada[l]).reshape(bsz, N_MOD, d)
        head, w1_g_b, w1_u_b, w1_down_b = _ffn_head(x2d, mod, w1_gu, w1_down, shift_row=0, layer=l)
        x2d, (w_in_b, w_out_b) = _ffn(
            x2d, mod, (w1_g_b, 0), (w1_u_b, 0), w1_down_b, shift_row=0, seq=seq, head=head,
            cast=(w_in, w_out), layer=l)
        proj, (w2_gu_b, w2_down_b) = _in_proj(
            x2d, mod, w_in_b, q_norm_w[l].reshape(1, HEAD_DIM), k_norm_w[l].reshape(1, HEAD_DIM),
            shift_row=3, seq=seq, cast=(w2_gu, w2_down), layer=l)
        x2d = _mixer(x2d, mod, proj, conv_w[l], w_out_b, gate_row=5, bsz=bsz, seq=seq, n_heads=n_heads)
        n_f = w2_down_b.shape[0] // FFN_TF
        x2d, _ = _ffn(x2d, mod, (w2_gu_b, 0), (w2_gu_b, n_f), w2_down_b, shift_row=6, seq=seq)
    return x2d.reshape(bsz, seq, d)
```

```python
import functools
import math

import jax
import jax.numpy as jnp
from jax import lax
from jax.experimental import pallas as pl
from jax.experimental.pallas import tpu as pltpu

F32 = jnp.float32
BF16 = jnp.bfloat16

HEAD_DIM = 128
CONV_K = 3
N_MOD = 9
FFN_RES = 0.5
EPS = 1e-6
LOG2E = 1.4426950408889634

F32_EXP2_UNDERFLOW = -150.0

V7X_VMEM_LIMIT_BYTES = 58 * 1024 * 1024
BF16_SUBLANE_TILE = 16

ADA_TN = 1024
FFN_TM = 1024
FFN_TF = 512
FFN_HEAD_TF = 256
PROJ_TM = 1024
PROJ_TN = 1024
ATT_T = 256
OUT_TN = 256
ROW_CHUNK = 256


def _params(semantics, vmem_bytes):
    return pltpu.CompilerParams(dimension_semantics=semantics,
                                vmem_limit_bytes=min(int(vmem_bytes), V7X_VMEM_LIMIT_BYTES))


def _rms_mod_rows(x_ref, mod_ref, h_ref, shift_row, rows):
    xv = x_ref[rows, :]
    ms = jnp.mean(xv * xv, axis=-1, keepdims=True)
    hn = xv * lax.rsqrt(ms + EPS)
    shift = mod_ref[0, shift_row:shift_row + 1, :]
    scale = mod_ref[0, shift_row + 1:shift_row + 2, :]
    h_ref[rows, :] = (hn * (1.0 + scale) + shift).astype(BF16)


def _adaln_kernel(c_ref, w_ref, b_ref, o_ref):
    cv = c_ref[...]
    c_act = (cv * jax.nn.sigmoid(cv)).astype(BF16)
    o_ref[...] = jnp.dot(c_act, w_ref[...].astype(BF16), preferred_element_type=F32) + b_ref[...]


def _adaln(c, w_ada, b_ada):
    bsz, d = c.shape
    n = w_ada.shape[1]
    return pl.pallas_call(
        _adaln_kernel,
        grid=(n // ADA_TN,),
        in_specs=[pl.BlockSpec((bsz, d), lambda j: (0, 0)),
                  pl.BlockSpec((d, ADA_TN), lambda j: (0, j)),
                  pl.BlockSpec((1, ADA_TN), lambda j: (0, j))],
        out_specs=pl.BlockSpec((bsz, ADA_TN), lambda j: (0, j)),
        out_shape=jax.ShapeDtypeStruct((bsz, n), F32),
        compiler_params=_params(("arbitrary",), 4 * d * ADA_TN * 4),
        name="adaln",
    )(c, w_ada, b_ada.reshape(1, n))


def _cast_blocking(rows, cols, n_i, n_j):
    lane_groups = cols // 128
    n_c = max(k for k in range(1, n_j + 1) if lane_groups % k == 0)
    assert rows % (n_i * BF16_SUBLANE_TILE) == 0 and cols % 128 == 0
    return rows // n_i, cols // n_c, n_c


def _side_cast_specs(cast, layer, n_i, n_j):
    blockings = [_cast_blocking(w.shape[1], w.shape[2], n_i, n_j) for w in cast]

    def cast_map(n_c, lead):
        return lambda i, j: lead + (i, jnp.minimum(j, n_c - 1))

    in_specs = [pl.BlockSpec((None, r, c), cast_map(n_c, (layer,))) for r, c, n_c in blockings]
    out_specs = [pl.BlockSpec((r, c), cast_map(n_c, ())) for r, c, n_c in blockings]
    out_shapes = [jax.ShapeDtypeStruct(w.shape[1:], BF16) for w in cast]
    return in_specs, out_specs, out_shapes, sum(2 * r * c * 6 for r, c, _ in blockings)


def _split_cast_refs(rest, n_cast):
    return rest[:n_cast], rest[n_cast], rest[n_cast + 1:2 * n_cast + 1], rest[2 * n_cast + 1:]


def _side_cast(cast_in, cast_out):
    for src_ref, dst_ref in zip(cast_in, cast_out):
        dst_ref[...] = src_ref[...].astype(BF16)


def _ffn_step(first, last, x_ref, mod_ref, h_ref, o_ref, weights, shift_row):
    if first:
        for r in range(x_ref.shape[0] // ROW_CHUNK):
            _rms_mod_rows(x_ref, mod_ref, h_ref, shift_row, pl.ds(r * ROW_CHUNK, ROW_CHUNK))
    wg, wu, wd = weights()
    h = h_ref[...]
    g = jnp.dot(h, wg, preferred_element_type=F32)
    u = jnp.dot(h, wu, preferred_element_type=F32)
    act = (g * jax.nn.sigmoid(g) * u).astype(BF16)
    part = jnp.dot(act, wd, preferred_element_type=F32)
    if first:
        o_ref[...] = part
    elif last:
        gate = FFN_RES * mod_ref[0, shift_row + 2:shift_row + 3, :]
        o_ref[...] = x_ref[...] + gate * (o_ref[...] + part)
    else:
        o_ref[...] += part


def _ffn_variants(active, f, n_f, step):
    def when(cond):
        return pl.when(cond if active is None else jnp.logical_and(active, cond))
    when(f == 0)(functools.partial(step, True, False))
    when(jnp.logical_and(f > 0, f < n_f - 1))(functools.partial(step, False, False))
    when(f == n_f - 1)(functools.partial(step, False, True))


def _ffn_head_kernel(x_ref, mod_ref, wg32_ref, wu32_ref, wd32_ref, o_ref, wg16_ref, wu16_ref, wd16_ref,
                     h_ref, *, shift_row, n_f):
    def weights():
        blocks = []
        for src_ref, dst_ref in ((wg32_ref, wg16_ref), (wu32_ref, wu16_ref), (wd32_ref, wd16_ref)):
            w = src_ref[...].astype(BF16)
            dst_ref[...] = w
            blocks.append(w)
        return blocks

    def step(first, last):
        _ffn_step(first, last, x_ref, mod_ref, h_ref, o_ref, weights, shift_row)

    _ffn_variants(None, pl.program_id(1), n_f, step)


def _ffn_head(x2d, mod, w_gu, w_down, *, shift_row, layer):
    d = x2d.shape[1]
    d_ff = w_down.shape[1]
    tm, tf = FFN_TM, FFN_HEAD_TF
    n_f = d_ff // tf
    assert n_f >= 3
    once = pl.Buffered(1)
    vmem = 2 * tm * d * 4 + tm * d * 2 + 2 * 3 * d * tf * 6 + 6 * tm * tf * 4 + (4 << 20)
    return pl.pallas_call(
        functools.partial(_ffn_head_kernel, shift_row=shift_row, n_f=n_f),
        grid=(1, n_f),
        in_specs=[pl.BlockSpec((tm, d), lambda i, f: (0, 0), pipeline_mode=once),
                  pl.BlockSpec((1, N_MOD, d), lambda i, f: (0, 0, 0)),
                  pl.BlockSpec((None, d, tf), lambda i, f: (layer, 0, f)),
                  pl.BlockSpec((None, d, tf), lambda i, f: (layer, 0, f + n_f)),
                  pl.BlockSpec((None, tf, d), lambda i, f: (layer, f, 0))],
        out_specs=[pl.BlockSpec((tm, d), lambda i, f: (0, 0)),
                   pl.BlockSpec((d, tf), lambda i, f: (0, f)),
                   pl.BlockSpec((d, tf), lambda i, f: (0, f)),
                   pl.BlockSpec((tf, d), lambda i, f: (f, 0))],
        out_shape=[jax.ShapeDtypeStruct((tm, d), F32),
                   jax.ShapeDtypeStruct((d, d_ff), BF16),
                   jax.ShapeDtypeStruct((d, d_ff), BF16),
                   jax.ShapeDtypeStruct((d_ff, d), BF16)],
        scratch_shapes=[pltpu.VMEM((tm, d), BF16)],
        compiler_params=_params(("arbitrary", "arbitrary"), vmem),
        name="ffn_head",
    )(x2d, mod, w_gu, w_gu, w_down)


def _ffn_kernel(x_ref, mod_ref, wg_ref, wu_ref, wd_ref, *rest, shift_row, n_f, n_cast, has_head):
    head_ref, rest = (rest[0], rest[1:]) if has_head else (None, rest)
    cast_in, o_ref, cast_out, (h_ref,) = _split_cast_refs(rest, n_cast)
    i = pl.program_id(0)
    f = pl.program_id(1)

    def step(first, last):
        _ffn_step(first, last, x_ref, mod_ref, h_ref, o_ref,
                  lambda: (wg_ref[...], wu_ref[...], wd_ref[...]), shift_row)
        _side_cast(cast_in, cast_out)

    _ffn_variants(i > 0 if has_head else None, f, n_f, step)

    if has_head:
        @pl.when(i == 0)
        def _():
            _side_cast(cast_in, cast_out)

        @pl.when(jnp.logical_and(i == 0, f == n_f - 1))
        def _():
            pltpu.sync_copy(head_ref, o_ref)


def _ffn(x2d, mod, w_g, w_u, w_down, *, shift_row, seq, head=None, cast=(), layer=0):
    t, d = x2d.shape
    d_ff = w_down.shape[0]
    n_f = d_ff // FFN_TF
    assert n_f >= 3
    tm = FFN_TM
    n_i = t // tm
    has_head = head is not None
    (wg_arr, g_off), (wu_arr, u_off) = w_g, w_u

    def col(f, i):
        return jnp.where(i == 0, 0, f) if has_head else f

    cast_in_specs, cast_out_specs, cast_shapes, cast_vmem = _side_cast_specs(cast, layer, n_i, n_f)
    vmem = ((2 * tm * d * 4) * 2 + tm * d * 2 + 2 * 3 * d * FFN_TF * 2 + 6 * tm * FFN_TF * 4
            + cast_vmem + (4 << 20))
    outs = pl.pallas_call(
        functools.partial(_ffn_kernel, shift_row=shift_row, n_f=n_f, n_cast=len(cast), has_head=has_head),
        grid=(n_i, n_f),
        in_specs=[pl.BlockSpec((tm, d), lambda i, f: (i, 0)),
                  pl.BlockSpec((1, N_MOD, d), lambda i, f: (i * tm // seq, 0, 0)),
                  pl.BlockSpec((d, FFN_TF), lambda i, f: (0, g_off + col(f, i))),
                  pl.BlockSpec((d, FFN_TF), lambda i, f: (0, u_off + col(f, i))),
                  pl.BlockSpec((FFN_TF, d), lambda i, f: (col(f, i), 0))]
                 + ([pl.BlockSpec(memory_space=pl.ANY)] if has_head else []) + cast_in_specs,
        out_specs=[pl.BlockSpec((tm, d), lambda i, f: (i, 0))] + cast_out_specs,
        out_shape=[jax.ShapeDtypeStruct((t, d), F32)] + cast_shapes,
        scratch_shapes=[pltpu.VMEM((tm, d), BF16)],
        compiler_params=_params(("arbitrary", "arbitrary"), vmem),
        name="ffn",
    )(x2d, mod, wg_arr, wu_arr, w_down, *((head,) if has_head else ()), *cast)
    return outs[0], outs[1:]


def _head_rms_gain(p, gain):
    outs = []
    for hd in range(p.shape[1] // HEAD_DIM):
        ph = p[:, hd * HEAD_DIM:(hd + 1) * HEAD_DIM]
        ms = jnp.mean(ph * ph, axis=-1, keepdims=True)
        outs.append(ph * lax.rsqrt(ms + EPS) * gain)
    return jnp.concatenate(outs, axis=1)


def _in_proj_kernel(x_ref, mod_ref, w_ref, qg_ref, kg_ref, *rest, shift_row, q_scale, n_cast):
    cast_in, o_ref, cast_out, (h_ref,) = _split_cast_refs(rest, n_cast)
    n = pl.program_id(1)
    tm = x_ref.shape[0]

    @pl.when(n == 0)
    def _():
        for r in range(tm // ROW_CHUNK):
            _rms_mod_rows(x_ref, mod_ref, h_ref, shift_row, pl.ds(r * ROW_CHUNK, ROW_CHUNK))
        p = jnp.dot(h_ref[...], w_ref[...], preferred_element_type=F32)
        o_ref[...] = _head_rms_gain(p, qg_ref[...] * q_scale).astype(BF16)
        _side_cast(cast_in, cast_out)

    @pl.when(n == 1)
    def _():
        p = jnp.dot(h_ref[...], w_ref[...], preferred_element_type=F32)
        o_ref[...] = _head_rms_gain(p, kg_ref[...]).astype(BF16)
        _side_cast(cast_in, cast_out)

    @pl.when(n >= 2)
    def _():
        o_ref[...] = jnp.dot(h_ref[...], w_ref[...], preferred_element_type=F32).astype(BF16)
        _side_cast(cast_in, cast_out)


def _in_proj(x2d, mod, w_in, q_gain, k_gain, *, shift_row, seq, cast=(), layer=0):
    t, d = x2d.shape
    n_cols = w_in.shape[1]
    tm = PROJ_TM
    n_i, n_n = t // tm, n_cols // PROJ_TN
    cast_in_specs, cast_out_specs, cast_shapes, cast_vmem = _side_cast_specs(cast, layer, n_i, n_n)
    vmem = (2 * tm * d * 4 + tm * d * 2 + 2 * d * PROJ_TN * 2 + 2 * tm * PROJ_TN * 2 + 6 * tm * PROJ_TN * 4
            + cast_vmem + (4 << 20))
    q_scale = LOG2E / math.sqrt(HEAD_DIM)
    outs = pl.pallas_call(
        functools.partial(_in_proj_kernel, shift_row=shift_row, q_scale=q_scale, n_cast=len(cast)),
        grid=(n_i, n_n),
        in_specs=[pl.BlockSpec((tm, d), lambda i, n: (i, 0)),
                  pl.BlockSpec((1, N_MOD, d), lambda i, n: (i * tm // seq, 0, 0)),
                  pl.BlockSpec((d, PROJ_TN), lambda i, n: (0, n)),
                  pl.BlockSpec((1, HEAD_DIM), lambda i, n: (0, 0)),
                  pl.BlockSpec((1, HEAD_DIM), lambda i, n: (0, 0))] + cast_in_specs,
        out_specs=[pl.BlockSpec((tm, PROJ_TN), lambda i, n: (i, n))] + cast_out_specs,
        out_shape=[jax.ShapeDtypeStruct((t, n_cols), BF16)] + cast_shapes,
        scratch_shapes=[pltpu.VMEM((tm, d), BF16)],
        compiler_params=_params(("arbitrary", "arbitrary"), vmem),
        name="in_proj",
    )(x2d, mod, w_in, q_gain, k_gain, *cast)
    return outs[0], outs[1:]


def _attn_step(chains, neg_tri, fillers=()):
    n_c = len(chains)
    z, sp16, rs, neg_incl, a, pv = ({} for _ in range(6))

    def scores(c):
        q, k = chains[c][0], chains[c][1]
        z[c] = lax.dot_general(q, k, (((1,), (1,)), ((), ())), preferred_element_type=F32)

    def softplus(c):
        mask = chains[c][3]
        sp = jnp.maximum(z[c], 0.0) + jnp.log(1.0 + jnp.exp2(jnp.minimum(z[c], -z[c]))) * LOG2E
        if mask is not None:
            sp = jnp.where(mask, sp, 0.0)
        sp16[c] = sp.astype(BF16)
        rs[c] = jnp.sum(sp, axis=-1, keepdims=True)

    def suffix_sum(c):
        neg_incl[c] = jnp.dot(sp16[c], neg_tri, preferred_element_type=F32)

    def weights(c):
        mask, carry_fn = chains[c][3], chains[c][4]
        w = jnp.exp2(z[c] + neg_incl[c] + carry_fn(rs))
        if mask is not None:
            w = jnp.where(mask, w, 0.0)
        a[c] = w.astype(BF16)

    def values(c):
        pv[c] = jnp.dot(a[c], chains[c][2], preferred_element_type=F32)

    stages = (scores, softplus, suffix_sum, weights, values)
    n_waves = n_c + len(stages) - 1
    done_fill = 0
    for wave in range(n_waves):
        for s in range(len(stages)):
            c = wave - s
            if 0 <= c < n_c:
                stages[s](c)
        while done_fill < (wave + 1) * len(fillers) // n_waves:
            fillers[done_fill]()
            done_fill += 1
    return [pv[c] for c in range(n_c)], [rs[c] for c in range(n_c)]


def _mixer_kernel(q_ref, k_ref, v_ref, x_ref, mod_ref, gb_ref, gc_ref, u_ref, gch_ref, uh_ref, cw_ref, w_ref,
                  o_ref, attn_ref, cu_ref, *, gate_row, nq):
    i = pl.program_id(1)
    t = q_ref.shape[0]
    n_h = q_ref.shape[1] // HEAD_DIM
    halo = gch_ref.shape[0]
    width = attn_ref.shape[1]
    d = o_ref.shape[1]

    def out_proj_chunks():
        attn_prev = attn_ref[...]
        prev = gch_ref[...].astype(F32) * uh_ref[...].astype(F32)
        cu_ref[0:halo, :] = jnp.where(i == 1, 0.0, prev)
        cu_ref[halo:halo + t, :] = gc_ref[...].astype(F32) * u_ref[...].astype(F32)
        conv = cu_ref[halo:halo + t, :] * cw_ref[CONV_K - 1:CONV_K, :]
        for k in range(CONV_K - 1):
            back = CONV_K - 1 - k
            conv = conv + cu_ref[halo - back:halo - back + t, :] * cw_ref[k:k + 1, :]
        y = (gb_ref[...].astype(F32) * conv).astype(BF16)
        gate = mod_ref[0, gate_row:gate_row + 1, :]

        def out_chunk(c):
            cols = slice(c * OUT_TN, (c + 1) * OUT_TN)
            mix = jnp.dot(attn_prev, w_ref[0:width, cols], preferred_element_type=F32)
            mix = mix + jnp.dot(y, w_ref[width:, cols], preferred_element_type=F32)
            o_ref[:, cols] = x_ref[:, cols] + gate[:, cols] * mix

        return [functools.partial(out_chunk, c) for c in range(d // OUT_TN)]

    def attention(fillers, has_prev):
        row = lax.broadcasted_iota(jnp.int32, (t, t), 0)
        col = lax.broadcasted_iota(jnp.int32, (t, t), 1)
        neg_tri = jnp.where(row >= col, -1.0, 0.0).astype(BF16)
        causal = col < row

        def lanes(hd):
            return slice(hd * HEAD_DIM, (hd + 1) * HEAD_DIM)

        def key_block(j):
            blk = pl.ds(pl.multiple_of(j * t, t), t)
            return [(k_ref[blk, lanes(hd)], v_ref[blk, lanes(hd)]) for hd in range(n_h)]

        qs = [q_ref[:, lanes(hd)] for hd in range(n_h)]
        zero = jnp.zeros((t, 1), F32)
        kv = key_block(i)
        chains = [(qs[hd], kv[hd][0], kv[hd][1], causal, lambda rs: zero) for hd in range(n_h)]
        if has_prev:
            kv = key_block(i - 1)
            chains += [(qs[hd], kv[hd][0], kv[hd][1], None, lambda rs, hd=hd: -rs[hd]) for hd in range(n_h)]
        pvs, rss = _attn_step(chains, neg_tri, fillers)
        if has_prev:
            accs = tuple(pvs[hd] + pvs[n_h + hd] for hd in range(n_h))
            carries = tuple(-rss[hd] - rss[n_h + hd] for hd in range(n_h))

            def live_max(carries):
                live = carries[0]
                for cr in carries[1:]:
                    live = jnp.maximum(live, cr)
                return jnp.max(live)

            def cond(state):
                jj, live = state[0], state[1]
                return jnp.logical_and(jj < i, live > F32_EXP2_UNDERFLOW)

            def body(state):
                jj, _, accs, carries = state
                kv = key_block(i - 1 - jj)
                pvs, rss = _attn_step([(qs[hd], kv[hd][0], kv[hd][1], None, lambda rs, hd=hd: carries[hd])
                                       for hd in range(n_h)], neg_tri)
                accs = tuple(acc + pv for acc, pv in zip(accs, pvs))
                carries = tuple(cr - rs for cr, rs in zip(carries, rss))
                return jj + 1, live_max(carries), accs, carries

            _, _, accs, _ = lax.while_loop(cond, body, (jnp.int32(1), live_max(carries), accs, carries))
        else:
            accs = pvs
        for hd in range(n_h):
            attn_ref[:, lanes(hd)] = accs[hd].astype(attn_ref.dtype)

    @pl.when(i == 0)
    def _():
        attention((), has_prev=False)

    @pl.when(jnp.logical_and(i > 0, i < nq))
    def _():
        attention(out_proj_chunks(), has_prev=True)

    @pl.when(i == nq)
    def _():
        for job in out_proj_chunks():
            job()


def _mixer(x2d, mod, proj, conv_w, w_out, *, gate_row, bsz, seq, n_heads):
    t_all, d = x2d.shape
    width = n_heads * HEAD_DIM
    t = ATT_T
    nq = seq // t
    halo = BF16_SUBLANE_TILE
    per = t // halo
    k_col, v_col, b_col, c_col, u_col = 1, 2, 3, 4, 5
    once = pl.Buffered(1)

    def out_row(b, i):
        return b * nq + jnp.maximum(i - 1, 0)

    def halo_map(col):
        return lambda b, i: (jnp.maximum(out_row(b, i) * per - 1, 0), col)

    vmem = (2 * seq * width * 2 + d * d * 2 + 2 * 2 * t * d * 4 + 2 * 4 * t * width * 2
            + (t + halo) * width * 4 + 2 * n_heads * 12 * t * t * 4 + (4 << 20))
    return pl.pallas_call(
        functools.partial(_mixer_kernel, gate_row=gate_row, nq=nq),
        grid=(bsz, nq + 1),
        in_specs=[pl.BlockSpec((t, width), lambda b, i: (b * nq + jnp.minimum(i, nq - 1), 0)),
                  pl.BlockSpec((seq, width), lambda b, i: (b, k_col), pipeline_mode=once),
                  pl.BlockSpec((seq, width), lambda b, i: (b, v_col), pipeline_mode=once),
                  pl.BlockSpec((t, d), lambda b, i: (out_row(b, i), 0)),
                  pl.BlockSpec((1, N_MOD, d), lambda b, i: (b, 0, 0)),
                  pl.BlockSpec((t, width), lambda b, i: (out_row(b, i), b_col)),
                  pl.BlockSpec((t, width), lambda b, i: (out_row(b, i), c_col)),
                  pl.BlockSpec((t, width), lambda b, i: (out_row(b, i), u_col)),
                  pl.BlockSpec((halo, width), halo_map(c_col)),
                  pl.BlockSpec((halo, width), halo_map(u_col)),
                  pl.BlockSpec((CONV_K, width), lambda b, i: (0, 0)),
                  pl.BlockSpec((d, d), lambda b, i: (0, 0))],
        out_specs=pl.BlockSpec((t, d), lambda b, i: (out_row(b, i), 0)),
        out_shape=jax.ShapeDtypeStruct((t_all, d), F32),
        scratch_shapes=[pltpu.VMEM((t, width), BF16), pltpu.VMEM((t + halo, width), F32)],
        compiler_params=_params(("arbitrary", "arbitrary"), vmem),
        name="mixer",
    )(proj, proj, proj, x2d, mod, proj, proj, proj, proj, proj, conv_w, w_out)


def kernel(x, c, w_ada, b_ada, w1_gu, w1_down, w_in, q_norm_w, k_norm_w, conv_w, w_out, w2_gu, w2_down):
    bsz, seq, d = x.shape
    depth = w_ada.shape[0]
    width = conv_w.shape[2]
    n_heads = width // HEAD_DIM
    assert w_in.shape[2] == 6 * width and width == PROJ_TN
    assert seq % max(FFN_TM, PROJ_TM, ATT_T) == 0 and d % OUT_TN == 0

    x2d = x.reshape(bsz * seq, d)
    for l in range(depth):
        mod = _adaln(c, w_ada[l], b_ada[l]).reshape(bsz, N_MOD, d)
        head, w1_g_b, w1_u_b, w1_down_b = _ffn_head(x2d, mod, w1_gu, w1_down, shift_row=0, layer=l)
        x2d, (w_in_b, w_out_b) = _ffn(
            x2d, mod, (w1_g_b, 0), (w1_u_b, 0), w1_down_b, shift_row=0, seq=seq, head=head,
            cast=(w_in, w_out), layer=l)
        proj, (w2_gu_b, w2_down_b) = _in_proj(
            x2d, mod, w_in_b, q_norm_w[l].reshape(1, HEAD_DIM), k_norm_w[l].reshape(1, HEAD_DIM),
            shift_row=3, seq=seq, cast=(w2_gu, w2_down), layer=l)
        x2d = _mixer(x2d, mod, proj, conv_w[l], w_out_b, gate_row=5, bsz=bsz, seq=seq, n_heads=n_heads)
        n_f = w2_down_b.shape[0] // FFN_TF
        x2d, _ = _ffn(x2d, mod, (w2_gu_b, 0), (w2_gu_b, n_f), w2_down_b, shift_row=6, seq=seq)
    return x2d.reshape(bsz, seq, d)
```

```python
import functools
import math

import jax
import jax.numpy as jnp
from jax import lax
from jax.experimental import pallas as pl
from jax.experimental.pallas import tpu as pltpu

F32 = jnp.float32
BF16 = jnp.bfloat16

HEAD_DIM = 128
K_COL, V_COL = 1, 2
CONV_K = 3
N_MOD = 9
FFN_RES = 0.5
EPS = 1e-6
LOG2E = 1.4426950408889634

F32_EXP2_UNDERFLOW = -150.0

V7X_VMEM_LIMIT_BYTES = 58 * 1024 * 1024
BF16_SUBLANE_TILE = 16

ADA_TN = 1024
FFN_TM = 1024
FFN_TF = 512
FFN_HEAD_TF = 256
PROJ_TM = 1024
PROJ_TN = 1024
ATT_T = 256
OUT_TN = 256
ROW_CHUNK = 256


def _params(semantics, vmem_bytes):
    return pltpu.CompilerParams(dimension_semantics=semantics,
                                vmem_limit_bytes=min(int(vmem_bytes), V7X_VMEM_LIMIT_BYTES))


def _rms_mod_rows(x_ref, mod_ref, h_ref, shift_row, rows):
    xv = x_ref[rows, :]
    ms = jnp.mean(xv * xv, axis=-1, keepdims=True)
    hn = xv * lax.rsqrt(ms + EPS)
    shift = mod_ref[0, shift_row:shift_row + 1, :]
    scale = mod_ref[0, shift_row + 1:shift_row + 2, :]
    h_ref[rows, :] = (hn * (1.0 + scale) + shift).astype(BF16)


def _adaln_kernel(c_ref, w_ref, b_ref, o_ref):
    cv = c_ref[...]
    c_act = (cv * jax.nn.sigmoid(cv)).astype(BF16)
    o_ref[...] = jnp.dot(c_act, w_ref[...].astype(BF16), preferred_element_type=F32) + b_ref[...]


def _adaln(c, w_ada, b_ada):
    bsz, d = c.shape
    n = w_ada.shape[1]
    return pl.pallas_call(
        _adaln_kernel,
        grid=(n // ADA_TN,),
        in_specs=[pl.BlockSpec((bsz, d), lambda j: (0, 0)),
                  pl.BlockSpec((d, ADA_TN), lambda j: (0, j)),
                  pl.BlockSpec((1, ADA_TN), lambda j: (0, j))],
        out_specs=pl.BlockSpec((bsz, ADA_TN), lambda j: (0, j)),
        out_shape=jax.ShapeDtypeStruct((bsz, n), F32),
        compiler_params=_params(("arbitrary",), 4 * d * ADA_TN * 4),
        name="adaln",
    )(c, w_ada, b_ada.reshape(1, n))


def _cast_blocking(rows, cols, n_i, n_j):
    lane_groups = cols // 128
    n_c = max(k for k in range(1, n_j + 1) if lane_groups % k == 0)
    assert rows % (n_i * BF16_SUBLANE_TILE) == 0 and cols % 128 == 0
    return rows // n_i, cols // n_c, n_c


def _side_cast_specs(cast, layer, n_i, n_j):
    blockings = [_cast_blocking(w.shape[1], w.shape[2], n_i, n_j) for w in cast]

    def cast_map(n_c, lead):
        return lambda i, j: lead + (i, jnp.minimum(j, n_c - 1))

    in_specs = [pl.BlockSpec((None, r, c), cast_map(n_c, (layer,))) for r, c, n_c in blockings]
    out_specs = [pl.BlockSpec((r, c), cast_map(n_c, ())) for r, c, n_c in blockings]
    out_shapes = [jax.ShapeDtypeStruct(w.shape[1:], BF16) for w in cast]
    return in_specs, out_specs, out_shapes, sum(2 * r * c * 6 for r, c, _ in blockings)


def _split_cast_refs(rest, n_cast):
    return rest[:n_cast], rest[n_cast], rest[n_cast + 1:2 * n_cast + 1], rest[2 * n_cast + 1:]


def _side_cast(cast_in, cast_out):
    for src_ref, dst_ref in zip(cast_in, cast_out):
        dst_ref[...] = src_ref[...].astype(BF16)


def _ffn_step(first, last, x_ref, mod_ref, h_ref, o_ref, weights, shift_row):
    if first:
        for r in range(x_ref.shape[0] // ROW_CHUNK):
            _rms_mod_rows(x_ref, mod_ref, h_ref, shift_row, pl.ds(r * ROW_CHUNK, ROW_CHUNK))
    wg, wu, wd = weights()
    h = h_ref[...]
    g = jnp.dot(h, wg, preferred_element_type=F32)
    u = jnp.dot(h, wu, preferred_element_type=F32)
    act = (g * jax.nn.sigmoid(g) * u).astype(BF16)
    part = jnp.dot(act, wd, preferred_element_type=F32)
    if first:
        o_ref[...] = part
    elif last:
        gate = FFN_RES * mod_ref[0, shift_row + 2:shift_row + 3, :]
        o_ref[...] = x_ref[...] + gate * (o_ref[...] + part)
    else:
        o_ref[...] += part


def _ffn_variants(active, f, n_f, step):
    def when(cond):
        return pl.when(cond if active is None else jnp.logical_and(active, cond))
    when(f == 0)(functools.partial(step, True, False))
    when(jnp.logical_and(f > 0, f < n_f - 1))(functools.partial(step, False, False))
    when(f == n_f - 1)(functools.partial(step, False, True))


def _ffn_head_kernel(x_ref, mod_ref, wg32_ref, wu32_ref, wd32_ref, o_ref, wg16_ref, wu16_ref, wd16_ref,
                     h_ref, *, shift_row, n_f):
    def weights():
        blocks = []
        for src_ref, dst_ref in ((wg32_ref, wg16_ref), (wu32_ref, wu16_ref), (wd32_ref, wd16_ref)):
            w = src_ref[...].astype(BF16)
            dst_ref[...] = w
            blocks.append(w)
        return blocks

    def step(first, last):
        _ffn_step(first, last, x_ref, mod_ref, h_ref, o_ref, weights, shift_row)

    _ffn_variants(None, pl.program_id(1), n_f, step)


def _ffn_head(x2d, mod, w_gu, w_down, *, shift_row, layer):
    d = x2d.shape[1]
    d_ff = w_down.shape[1]
    tm, tf = FFN_TM, FFN_HEAD_TF
    n_f = d_ff // tf
    assert n_f >= 3
    once = pl.Buffered(1)
    vmem = 2 * tm * d * 4 + tm * d * 2 + 2 * 3 * d * tf * 6 + 6 * tm * tf * 4 + (4 << 20)
    return pl.pallas_call(
        functools.partial(_ffn_head_kernel, shift_row=shift_row, n_f=n_f),
        grid=(1, n_f),
        in_specs=[pl.BlockSpec((tm, d), lambda i, f: (0, 0), pipeline_mode=once),
                  pl.BlockSpec((1, N_MOD, d), lambda i, f: (0, 0, 0)),
                  pl.BlockSpec((None, d, tf), lambda i, f: (layer, 0, f)),
                  pl.BlockSpec((None, d, tf), lambda i, f: (layer, 0, f + n_f)),
                  pl.BlockSpec((None, tf, d), lambda i, f: (layer, f, 0))],
        out_specs=[pl.BlockSpec((tm, d), lambda i, f: (0, 0)),
                   pl.BlockSpec((d, tf), lambda i, f: (0, f)),
                   pl.BlockSpec((d, tf), lambda i, f: (0, f)),
                   pl.BlockSpec((tf, d), lambda i, f: (f, 0))],
        out_shape=[jax.ShapeDtypeStruct((tm, d), F32),
                   jax.ShapeDtypeStruct((d, d_ff), BF16),
                   jax.ShapeDtypeStruct((d, d_ff), BF16),
                   jax.ShapeDtypeStruct((d_ff, d), BF16)],
        scratch_shapes=[pltpu.VMEM((tm, d), BF16)],
        compiler_params=_params(("arbitrary", "arbitrary"), vmem),
        name="ffn_head",
    )(x2d, mod, w_gu, w_gu, w_down)


def _ffn_kernel(x_ref, mod_ref, wg_ref, wu_ref, wd_ref, *rest, shift_row, n_f, n_cast, has_head):
    head_ref, rest = (rest[0], rest[1:]) if has_head else (None, rest)
    cast_in, o_ref, cast_out, (h_ref,) = _split_cast_refs(rest, n_cast)
    i = pl.program_id(0)
    f = pl.program_id(1)

    def step(first, last):
        _ffn_step(first, last, x_ref, mod_ref, h_ref, o_ref,
                  lambda: (wg_ref[...], wu_ref[...], wd_ref[...]), shift_row)
        _side_cast(cast_in, cast_out)

    _ffn_variants(i > 0 if has_head else None, f, n_f, step)

    if has_head:
        @pl.when(i == 0)
        def _():
            _side_cast(cast_in, cast_out)

        @pl.when(jnp.logical_and(i == 0, f == n_f - 1))
        def _():
            pltpu.sync_copy(head_ref, o_ref)


def _ffn(x2d, mod, w_g, w_u, w_down, *, shift_row, seq, head=None, cast=(), layer=0):
    t, d = x2d.shape
    d_ff = w_down.shape[0]
    n_f = d_ff // FFN_TF
    assert n_f >= 3
    tm = FFN_TM
    n_i = t // tm
    has_head = head is not None
    (wg_arr, g_off), (wu_arr, u_off) = w_g, w_u

    def col(f, i):
        return jnp.where(i == 0, 0, f) if has_head else f

    cast_in_specs, cast_out_specs, cast_shapes, cast_vmem = _side_cast_specs(cast, layer, n_i, n_f)
    vmem = ((2 * tm * d * 4) * 2 + tm * d * 2 + 2 * 3 * d * FFN_TF * 2 + 6 * tm * FFN_TF * 4
            + cast_vmem + (4 << 20))
    outs = pl.pallas_call(
        functools.partial(_ffn_kernel, shift_row=shift_row, n_f=n_f, n_cast=len(cast), has_head=has_head),
        grid=(n_i, n_f),
        in_specs=[pl.BlockSpec((tm, d), lambda i, f: (i, 0)),
                  pl.BlockSpec((1, N_MOD, d), lambda i, f: (i * tm // seq, 0, 0)),
                  pl.BlockSpec((d, FFN_TF), lambda i, f: (0, g_off + col(f, i))),
                  pl.BlockSpec((d, FFN_TF), lambda i, f: (0, u_off + col(f, i))),
                  pl.BlockSpec((FFN_TF, d), lambda i, f: (col(f, i), 0))]
                 + ([pl.BlockSpec(memory_space=pl.ANY)] if has_head else []) + cast_in_specs,
        out_specs=[pl.BlockSpec((tm, d), lambda i, f: (i, 0))] + cast_out_specs,
        out_shape=[jax.ShapeDtypeStruct((t, d), F32)] + cast_shapes,
        scratch_shapes=[pltpu.VMEM((tm, d), BF16)],
        compiler_params=_params(("arbitrary", "arbitrary"), vmem),
        name="ffn",
    )(x2d, mod, wg_arr, wu_arr, w_down, *((head,) if has_head else ()), *cast)
    return outs[0], outs[1:]


def _head_rms_gain(p, gain):
    outs = []
    for hd in range(p.shape[1] // HEAD_DIM):
        ph = p[:, hd * HEAD_DIM:(hd + 1) * HEAD_DIM]
        ms = jnp.mean(ph * ph, axis=-1, keepdims=True)
        outs.append(ph * lax.rsqrt(ms + EPS) * gain)
    return jnp.concatenate(outs, axis=1)


def _in_proj_kernel(x_ref, mod_ref, w_ref, qg_ref, kg_ref, *rest, shift_row, q_scale, n_cast):
    cast_in, o_ref, cast_out, (h_ref,) = _split_cast_refs(rest, n_cast)
    n = pl.program_id(1)
    tm = x_ref.shape[0]

    @pl.when(n == 0)
    def _():
        for r in range(tm // ROW_CHUNK):
            _rms_mod_rows(x_ref, mod_ref, h_ref, shift_row, pl.ds(r * ROW_CHUNK, ROW_CHUNK))
        p = jnp.dot(h_ref[...], w_ref[...], preferred_element_type=F32)
        o_ref[...] = _head_rms_gain(p, qg_ref[...] * q_scale).astype(BF16)
        _side_cast(cast_in, cast_out)

    @pl.when(n == 1)
    def _():
        p = jnp.dot(h_ref[...], w_ref[...], preferred_element_type=F32)
        o_ref[...] = _head_rms_gain(p, kg_ref[...]).astype(BF16)
        _side_cast(cast_in, cast_out)

    @pl.when(n >= 2)
    def _():
        o_ref[...] = jnp.dot(h_ref[...], w_ref[...], preferred_element_type=F32).astype(BF16)
        _side_cast(cast_in, cast_out)


def _in_proj(x2d, mod, w_in, q_gain, k_gain, *, shift_row, seq, cast=(), layer=0):
    t, d = x2d.shape
    n_cols = w_in.shape[1]
    tm = PROJ_TM
    n_i, n_n = t // tm, n_cols // PROJ_TN
    cast_in_specs, cast_out_specs, cast_shapes, cast_vmem = _side_cast_specs(cast, layer, n_i, n_n)
    vmem = (2 * tm * d * 4 + tm * d * 2 + 2 * d * PROJ_TN * 2 + 2 * tm * PROJ_TN * 2 + 6 * tm * PROJ_TN * 4
            + cast_vmem + (4 << 20))
    q_scale = LOG2E / math.sqrt(HEAD_DIM)
    outs = pl.pallas_call(
        functools.partial(_in_proj_kernel, shift_row=shift_row, q_scale=q_scale, n_cast=len(cast)),
        grid=(n_i, n_n),
        in_specs=[pl.BlockSpec((tm, d), lambda i, n: (i, 0)),
                  pl.BlockSpec((1, N_MOD, d), lambda i, n: (i * tm // seq, 0, 0)),
                  pl.BlockSpec((d, PROJ_TN), lambda i, n: (0, n)),
                  pl.BlockSpec((1, HEAD_DIM), lambda i, n: (0, 0)),
                  pl.BlockSpec((1, HEAD_DIM), lambda i, n: (0, 0))] + cast_in_specs,
        out_specs=[pl.BlockSpec((tm, PROJ_TN), lambda i, n: (i, n))] + cast_out_specs,
        out_shape=[jax.ShapeDtypeStruct((t, n_cols), BF16)] + cast_shapes,
        scratch_shapes=[pltpu.VMEM((tm, d), BF16)],
        compiler_params=_params(("arbitrary", "arbitrary"), vmem),
        name="in_proj",
    )(x2d, mod, w_in, q_gain, k_gain, *cast)
    return outs[0], outs[1:]


def _attn_step(chains, neg_tri, fillers=()):
    n_c = len(chains)
    z, sp16, rs, neg_incl, a, pv = ({} for _ in range(6))

    def scores(c):
        q, k = chains[c][0], chains[c][1]
        z[c] = lax.dot_general(q, k, (((1,), (1,)), ((), ())), preferred_element_type=F32)

    def softplus(c):
        mask = chains[c][3]
        sp = jnp.maximum(z[c], 0.0) + jnp.log(1.0 + jnp.exp2(jnp.minimum(z[c], -z[c]))) * LOG2E
        if mask is not None:
            sp = jnp.where(mask, sp, 0.0)
        sp16[c] = sp.astype(BF16)
        rs[c] = jnp.sum(sp, axis=-1, keepdims=True)

    def suffix_sum(c):
        neg_incl[c] = jnp.dot(sp16[c], neg_tri, preferred_element_type=F32)

    def weights(c):
        mask, carry_fn = chains[c][3], chains[c][4]
        w = jnp.exp2(z[c] + neg_incl[c] + carry_fn(rs))
        if mask is not None:
            w = jnp.where(mask, w, 0.0)
        a[c] = w.astype(BF16)

    def values(c):
        pv[c] = jnp.dot(a[c], chains[c][2], preferred_element_type=F32)

    stages = (scores, softplus, suffix_sum, weights, values)
    n_waves = n_c + len(stages) - 1
    done_fill = 0
    for wave in range(n_waves):
        for s in range(len(stages)):
            c = wave - s
            if 0 <= c < n_c:
                stages[s](c)
        while done_fill < (wave + 1) * len(fillers) // n_waves:
            fillers[done_fill]()
            done_fill += 1
    return [pv[c] for c in range(n_c)], [rs[c] for c in range(n_c)]


def _mixer_kernel(q_ref, kd_ref, kp_ref, vd_ref, vp_ref, kv_hbm, x_ref, mod_ref, gb_ref, gc_ref, u_ref,
                  gch_ref, uh_ref, cw_ref, w_ref, *rest, gate_row, nq, n_cast):
    cast_in, o_ref, cast_out, (attn_ref, cu_ref, kbuf_ref, vbuf_ref, kv_sem) = _split_cast_refs(rest, n_cast)
    b = pl.program_id(0)
    i = pl.program_id(1)
    t = q_ref.shape[0]
    n_h = q_ref.shape[1] // HEAD_DIM
    halo = gch_ref.shape[0]
    width = attn_ref.shape[1]
    d = o_ref.shape[1]

    def out_proj_chunks():
        attn_prev = attn_ref[...]
        prev = gch_ref[...].astype(F32) * uh_ref[...].astype(F32)
        cu_ref[0:halo, :] = jnp.where(i == 1, 0.0, prev)
        cu_ref[halo:halo + t, :] = gc_ref[...].astype(F32) * u_ref[...].astype(F32)
        conv = cu_ref[halo:halo + t, :] * cw_ref[CONV_K - 1:CONV_K, :]
        for k in range(CONV_K - 1):
            back = CONV_K - 1 - k
            conv = conv + cu_ref[halo - back:halo - back + t, :] * cw_ref[k:k + 1, :]
        y = (gb_ref[...].astype(F32) * conv).astype(BF16)
        gate = mod_ref[0, gate_row:gate_row + 1, :]

        def out_chunk(c):
            cols = slice(c * OUT_TN, (c + 1) * OUT_TN)
            mix = jnp.dot(attn_prev, w_ref[0:width, cols], preferred_element_type=F32)
            mix = mix + jnp.dot(y, w_ref[width:, cols], preferred_element_type=F32)
            o_ref[:, cols] = x_ref[:, cols] + gate[:, cols] * mix

        return [functools.partial(out_chunk, c) for c in range(d // OUT_TN)]

    def attention(fillers, has_prev):
        row = lax.broadcasted_iota(jnp.int32, (t, t), 0)
        col = lax.broadcasted_iota(jnp.int32, (t, t), 1)
        neg_tri = jnp.where(row >= col, -1.0, 0.0).astype(BF16)
        causal = col < row

        def lanes(hd):
            return slice(hd * HEAD_DIM, (hd + 1) * HEAD_DIM)

        def heads_of(kblk_ref, vblk_ref):
            return [(kblk_ref[:, lanes(hd)], vblk_ref[:, lanes(hd)]) for hd in range(n_h)]

        def fetch_key_block(j):
            rows = pl.ds(pl.multiple_of((b * nq + j) * t, t), t)
            copies = [pltpu.make_async_copy(kv_hbm.at[rows, pl.ds(col * width, width)], buf, kv_sem.at[n])
                      for n, (col, buf) in enumerate(((K_COL, kbuf_ref), (V_COL, vbuf_ref)))]
            for cp in copies:
                cp.start()
            for cp in copies:
                cp.wait()

        qs = [q_ref[:, lanes(hd)] for hd in range(n_h)]
        zero = jnp.zeros((t, 1), F32)
        kv = heads_of(kd_ref, vd_ref)
        chains = [(qs[hd], kv[hd][0], kv[hd][1], causal, lambda rs: zero) for hd in range(n_h)]
        if has_prev:
            kv = heads_of(kp_ref, vp_ref)
            chains += [(qs[hd], kv[hd][0], kv[hd][1], None, lambda rs, hd=hd: -rs[hd]) for hd in range(n_h)]
        pvs, rss = _attn_step(chains, neg_tri, fillers)
        if has_prev:
            accs = tuple(pvs[hd] + pvs[n_h + hd] for hd in range(n_h))
            carries = tuple(-rss[hd] - rss[n_h + hd] for hd in range(n_h))

            def live_max(carries):
                live = carries[0]
                for cr in carries[1:]:
                    live = jnp.maximum(live, cr)
                return jnp.max(live)

            def cond(state):
                jj, live = state[0], state[1]
                return jnp.logical_and(jj < i, live > F32_EXP2_UNDERFLOW)

            def body(state):
                jj, _, accs, carries = state
                fetch_key_block(i - 1 - jj)
                kv = heads_of(kbuf_ref, vbuf_ref)
                pvs, rss = _attn_step([(qs[hd], kv[hd][0], kv[hd][1], None, lambda rs, hd=hd: carries[hd])
                                       for hd in range(n_h)], neg_tri)
                accs = tuple(acc + pv for acc, pv in zip(accs, pvs))
                carries = tuple(cr - rs for cr, rs in zip(carries, rss))
                return jj + 1, live_max(carries), accs, carries

            _, _, accs, _ = lax.while_loop(cond, body, (jnp.int32(1), live_max(carries), accs, carries))
        else:
            accs = pvs
        for hd in range(n_h):
            attn_ref[:, lanes(hd)] = accs[hd].astype(attn_ref.dtype)

    @pl.when(i == 0)
    def _():
        attention((), has_prev=False)
        _side_cast(cast_in, cast_out)

    @pl.when(jnp.logical_and(i > 0, i < nq))
    def _():
        attention(out_proj_chunks(), has_prev=True)
        _side_cast(cast_in, cast_out)

    @pl.when(i == nq)
    def _():
        for job in out_proj_chunks():
            job()
        _side_cast(cast_in, cast_out)


def _mixer(x2d, mod, proj, conv_w, w_out, *, gate_row, bsz, seq, n_heads, cast=(), layer=0):
    t_all, d = x2d.shape
    width = n_heads * HEAD_DIM
    t = ATT_T
    nq = seq // t
    halo = BF16_SUBLANE_TILE
    per = t // halo
    b_col, c_col, u_col = 3, 4, 5

    def q_row(b, i):
        return b * nq + jnp.minimum(i, nq - 1)

    def prev_row(b, i):
        return b * nq + jnp.maximum(jnp.minimum(i, nq - 1) - 1, 0)

    def out_row(b, i):
        return b * nq + jnp.maximum(i - 1, 0)

    def halo_map(col):
        return lambda b, i: (jnp.maximum(out_row(b, i) * per - 1, 0), col)

    cast_in_specs, cast_out_specs, cast_shapes, cast_vmem = _side_cast_specs(cast, layer, bsz, nq + 1)
    vmem = (2 * 5 * t * width * 2 + 2 * t * width * 2 + d * d * 2 + 2 * 2 * t * d * 4 + 2 * 4 * t * width * 2
            + (t + halo) * width * 4 + 2 * n_heads * 12 * t * t * 4 + cast_vmem + (4 << 20))
    outs = pl.pallas_call(
        functools.partial(_mixer_kernel, gate_row=gate_row, nq=nq, n_cast=len(cast)),
        grid=(bsz, nq + 1),
        in_specs=[pl.BlockSpec((t, width), lambda b, i: (q_row(b, i), 0)),
                  pl.BlockSpec((t, width), lambda b, i: (q_row(b, i), K_COL)),
                  pl.BlockSpec((t, width), lambda b, i: (prev_row(b, i), K_COL)),
                  pl.BlockSpec((t, width), lambda b, i: (q_row(b, i), V_COL)),
                  pl.BlockSpec((t, width), lambda b, i: (prev_row(b, i), V_COL)),
                  pl.BlockSpec(memory_space=pl.ANY),
                  pl.BlockSpec((t, d), lambda b, i: (out_row(b, i), 0)),
                  pl.BlockSpec((1, N_MOD, d), lambda b, i: (b, 0, 0)),
                  pl.BlockSpec((t, width), lambda b, i: (out_row(b, i), b_col)),
                  pl.BlockSpec((t, width), lambda b, i: (out_row(b, i), c_col)),
                  pl.BlockSpec((t, width), lambda b, i: (out_row(b, i), u_col)),
                  pl.BlockSpec((halo, width), halo_map(c_col)),
                  pl.BlockSpec((halo, width), halo_map(u_col)),
                  pl.BlockSpec((CONV_K, width), lambda b, i: (0, 0)),
                  pl.BlockSpec((d, d), lambda b, i: (0, 0))] + cast_in_specs,
        out_specs=[pl.BlockSpec((t, d), lambda b, i: (out_row(b, i), 0))] + cast_out_specs,
        out_shape=[jax.ShapeDtypeStruct((t_all, d), F32)] + cast_shapes,
        scratch_shapes=[pltpu.VMEM((t, width), BF16), pltpu.VMEM((t + halo, width), F32),
                        pltpu.VMEM((t, width), BF16), pltpu.VMEM((t, width), BF16),
                        pltpu.SemaphoreType.DMA((2,))],
        compiler_params=_params(("arbitrary", "arbitrary"), vmem),
        name="mixer",
    )(proj, proj, proj, proj, proj, proj, x2d, mod, proj, proj, proj, proj, proj, conv_w, w_out, *cast)
    return outs[0], outs[1:]


def kernel(x, c, w_ada, b_ada, w1_gu, w1_down, w_in, q_norm_w, k_norm_w, conv_w, w_out, w2_gu, w2_down):
    bsz, seq, d = x.shape
    depth = w_ada.shape[0]
    width = conv_w.shape[2]
    n_heads = width // HEAD_DIM
    assert w_in.shape[2] == 6 * width and width == PROJ_TN
    assert seq % max(FFN_TM, PROJ_TM, ATT_T) == 0 and d % OUT_TN == 0

    x2d = x.reshape(bsz * seq, d)
    for l in range(depth):
        mod = _adaln(c, w_ada[l], b_ada[l]).reshape(bsz, N_MOD, d)
        head, w1_g_b, w1_u_b, w1_down_b = _ffn_head(x2d, mod, w1_gu, w1_down, shift_row=0, layer=l)
        x2d, (w_in_b, w_out_b) = _ffn(
            x2d, mod, (w1_g_b, 0), (w1_u_b, 0), w1_down_b, shift_row=0, seq=seq, head=head,
            cast=(w_in, w_out), layer=l)
        proj, (w2_gu_b,) = _in_proj(
            x2d, mod, w_in_b, q_norm_w[l].reshape(1, HEAD_DIM), k_norm_w[l].reshape(1, HEAD_DIM),
            shift_row=3, seq=seq, cast=(w2_gu,), layer=l)
        x2d, (w2_down_b,) = _mixer(x2d, mod, proj, conv_w[l], w_out_b, gate_row=5, bsz=bsz, seq=seq,
                                   n_heads=n_heads, cast=(w2_down,), layer=l)
        n_f = w2_down_b.shape[0] // FFN_TF
        x2d, _ = _ffn(x2d, mod, (w2_gu_b, 0), (w2_gu_b, n_f), w2_down_b, shift_row=6, seq=seq)
    return x2d.reshape(bsz, seq, d)
```

```python
import functools
import math

import jax
import jax.numpy as jnp
from jax import lax
from jax.experimental import pallas as pl
from jax.experimental.pallas import tpu as pltpu

F32 = jnp.float32
BF16 = jnp.bfloat16

HEAD_DIM = 128
K_COL, V_COL = 1, 2
CONV_K = 3
N_MOD = 9
FFN_RES = 0.5
EPS = 1e-6
LOG2E = 1.4426950408889634

F32_EXP2_UNDERFLOW = -150.0

V7X_VMEM_LIMIT_BYTES = 58 * 1024 * 1024
BF16_SUBLANE_TILE = 16

ADA_TN = 1024
FFN_TM = 1024
FFN_TF = 512
FFN_HEAD_TF = 256
PROJ_TM = 1024
PROJ_TN = 1024
ATT_T = 256
OUT_TN = 256
ROW_CHUNK = 256


def _params(semantics, vmem_bytes):
    return pltpu.CompilerParams(dimension_semantics=semantics,
                                vmem_limit_bytes=min(int(vmem_bytes), V7X_VMEM_LIMIT_BYTES))


def _rms_mod_rows(x_ref, mod_ref, h_ref, shift_row, rows):
    xv = x_ref[rows, :]
    ms = jnp.mean(xv * xv, axis=-1, keepdims=True)
    hn = xv * lax.rsqrt(ms + EPS)
    shift = mod_ref[0, shift_row:shift_row + 1, :]
    scale = mod_ref[0, shift_row + 1:shift_row + 2, :]
    h_ref[rows, :] = (hn * (1.0 + scale) + shift).astype(BF16)


def _adaln_kernel(c_ref, w_ref, b_ref, o_ref):
    cv = c_ref[...]
    c_act = (cv * jax.nn.sigmoid(cv)).astype(BF16)
    o_ref[...] = jnp.dot(c_act, w_ref[...].astype(BF16), preferred_element_type=F32) + b_ref[...]


def _adaln(c, w_ada, b_ada):
    bsz, d = c.shape
    n = w_ada.shape[1]
    return pl.pallas_call(
        _adaln_kernel,
        grid=(n // ADA_TN,),
        in_specs=[pl.BlockSpec((bsz, d), lambda j: (0, 0)),
                  pl.BlockSpec((d, ADA_TN), lambda j: (0, j)),
                  pl.BlockSpec((1, ADA_TN), lambda j: (0, j))],
        out_specs=pl.BlockSpec((bsz, ADA_TN), lambda j: (0, j)),
        out_shape=jax.ShapeDtypeStruct((bsz, n), F32),
        compiler_params=_params(("arbitrary",), 4 * d * ADA_TN * 4),
        name="adaln",
    )(c, w_ada, b_ada.reshape(1, n))


def _cast_blocking(rows, cols, n_i, n_j):
    lane_groups = cols // 128
    n_c = max(k for k in range(1, n_j + 1) if lane_groups % k == 0)
    assert rows % (n_i * BF16_SUBLANE_TILE) == 0 and cols % 128 == 0
    return rows // n_i, cols // n_c, n_c


def _side_cast_specs(cast, layer, n_i, n_j):
    blockings = [_cast_blocking(w.shape[1], w.shape[2], n_i, n_j) for w in cast]

    def cast_map(n_c, lead):
        return lambda i, j: lead + (i, jnp.minimum(j, n_c - 1))

    in_specs = [pl.BlockSpec((None, r, c), cast_map(n_c, (layer,))) for r, c, n_c in blockings]
    out_specs = [pl.BlockSpec((r, c), cast_map(n_c, ())) for r, c, n_c in blockings]
    out_shapes = [jax.ShapeDtypeStruct(w.shape[1:], BF16) for w in cast]
    return in_specs, out_specs, out_shapes, sum(2 * r * c * 6 for r, c, _ in blockings)


def _split_cast_refs(rest, n_cast):
    return rest[:n_cast], rest[n_cast], rest[n_cast + 1:2 * n_cast + 1], rest[2 * n_cast + 1:]


def _side_cast(cast_in, cast_out):
    for src_ref, dst_ref in zip(cast_in, cast_out):
        dst_ref[...] = src_ref[...].astype(BF16)


def _ffn_step(first, last, x_ref, mod_ref, h_ref, o_ref, weights, shift_row):
    if first:
        for r in range(x_ref.shape[0] // ROW_CHUNK):
            _rms_mod_rows(x_ref, mod_ref, h_ref, shift_row, pl.ds(r * ROW_CHUNK, ROW_CHUNK))
    wg, wu, wd = weights()
    h = h_ref[...]
    g = jnp.dot(h, wg, preferred_element_type=F32)
    u = jnp.dot(h, wu, preferred_element_type=F32)
    act = (g * jax.nn.sigmoid(g) * u).astype(BF16)
    part = jnp.dot(act, wd, preferred_element_type=F32)
    if first:
        o_ref[...] = part
    elif last:
        gate = FFN_RES * mod_ref[0, shift_row + 2:shift_row + 3, :]
        o_ref[...] = x_ref[...] + gate * (o_ref[...] + part)
    else:
        o_ref[...] += part


def _ffn_variants(active, f, n_f, step):
    def when(cond):
        return pl.when(cond if active is None else jnp.logical_and(active, cond))
    when(f == 0)(functools.partial(step, True, False))
    when(jnp.logical_and(f > 0, f < n_f - 1))(functools.partial(step, False, False))
    when(f == n_f - 1)(functools.partial(step, False, True))


def _ffn_head_kernel(x_ref, mod_ref, wg32_ref, wu32_ref, wd32_ref, o_ref, wg16_ref, wu16_ref, wd16_ref,
                     h_ref, *, shift_row, n_f):
    def weights():
        blocks = []
        for src_ref, dst_ref in ((wg32_ref, wg16_ref), (wu32_ref, wu16_ref), (wd32_ref, wd16_ref)):
            w = src_ref[...].astype(BF16)
            dst_ref[...] = w
            blocks.append(w)
        return blocks

    def step(first, last):
        _ffn_step(first, last, x_ref, mod_ref, h_ref, o_ref, weights, shift_row)

    _ffn_variants(None, pl.program_id(1), n_f, step)


def _ffn_head(x2d, mod, w_gu, w_down, *, shift_row, layer):
    d = x2d.shape[1]
    d_ff = w_down.shape[1]
    tm, tf = FFN_TM, FFN_HEAD_TF
    n_f = d_ff // tf
    assert n_f >= 3
    once = pl.Buffered(1)
    vmem = 2 * tm * d * 4 + tm * d * 2 + 2 * 3 * d * tf * 6 + 6 * tm * tf * 4 + (4 << 20)
    return pl.pallas_call(
        functools.partial(_ffn_head_kernel, shift_row=shift_row, n_f=n_f),
        grid=(1, n_f),
        in_specs=[pl.BlockSpec((tm, d), lambda i, f: (0, 0), pipeline_mode=once),
                  pl.BlockSpec((1, N_MOD, d), lambda i, f: (0, 0, 0)),
                  pl.BlockSpec((None, d, tf), lambda i, f: (layer, 0, f)),
                  pl.BlockSpec((None, d, tf), lambda i, f: (layer, 0, f + n_f)),
                  pl.BlockSpec((None, tf, d), lambda i, f: (layer, f, 0))],
        out_specs=[pl.BlockSpec((tm, d), lambda i, f: (0, 0)),
                   pl.BlockSpec((d, tf), lambda i, f: (0, f)),
                   pl.BlockSpec((d, tf), lambda i, f: (0, f)),
                   pl.BlockSpec((tf, d), lambda i, f: (f, 0))],
        out_shape=[jax.ShapeDtypeStruct((tm, d), F32),
                   jax.ShapeDtypeStruct((d, d_ff), BF16),
                   jax.ShapeDtypeStruct((d, d_ff), BF16),
                   jax.ShapeDtypeStruct((d_ff, d), BF16)],
        scratch_shapes=[pltpu.VMEM((tm, d), BF16)],
        compiler_params=_params(("arbitrary", "arbitrary"), vmem),
        name="ffn_head",
    )(x2d, mod, w_gu, w_gu, w_down)


def _ffn_kernel(x_ref, mod_ref, wg_ref, wu_ref, wd_ref, *rest, shift_row, n_f, n_cast, has_head):
    head_ref, rest = (rest[0], rest[1:]) if has_head else (None, rest)
    cast_in, o_ref, cast_out, (h_ref,) = _split_cast_refs(rest, n_cast)
    i = pl.program_id(0)
    f = pl.program_id(1)

    def step(first, last):
        _ffn_step(first, last, x_ref, mod_ref, h_ref, o_ref,
                  lambda: (wg_ref[...], wu_ref[...], wd_ref[...]), shift_row)
        _side_cast(cast_in, cast_out)

    _ffn_variants(i > 0 if has_head else None, f, n_f, step)

    if has_head:
        @pl.when(i == 0)
        def _():
            _side_cast(cast_in, cast_out)

        @pl.when(jnp.logical_and(i == 0, f == n_f - 1))
        def _():
            pltpu.sync_copy(head_ref, o_ref)


def _ffn(x2d, mod, w_g, w_u, w_down, *, shift_row, seq, head=None, cast=(), layer=0):
    t, d = x2d.shape
    d_ff = w_down.shape[0]
    n_f = d_ff // FFN_TF
    assert n_f >= 3
    tm = FFN_TM
    n_i = t // tm
    has_head = head is not None
    (wg_arr, g_off), (wu_arr, u_off) = w_g, w_u

    def col(f, i):
        return jnp.where(i == 0, 0, f) if has_head else f

    cast_in_specs, cast_out_specs, cast_shapes, cast_vmem = _side_cast_specs(cast, layer, n_i, n_f)
    vmem = ((2 * tm * d * 4) * 2 + tm * d * 2 + 2 * 3 * d * FFN_TF * 2 + 6 * tm * FFN_TF * 4
            + cast_vmem + (4 << 20))
    outs = pl.pallas_call(
        functools.partial(_ffn_kernel, shift_row=shift_row, n_f=n_f, n_cast=len(cast), has_head=has_head),
        grid=(n_i, n_f),
        in_specs=[pl.BlockSpec((tm, d), lambda i, f: (i, 0)),
                  pl.BlockSpec((1, N_MOD, d), lambda i, f: (i * tm // seq, 0, 0)),
                  pl.BlockSpec((d, FFN_TF), lambda i, f: (0, g_off + col(f, i))),
                  pl.BlockSpec((d, FFN_TF), lambda i, f: (0, u_off + col(f, i))),
                  pl.BlockSpec((FFN_TF, d), lambda i, f: (col(f, i), 0))]
                 + ([pl.BlockSpec(memory_space=pl.ANY)] if has_head else []) + cast_in_specs,
        out_specs=[pl.BlockSpec((tm, d), lambda i, f: (i, 0))] + cast_out_specs,
        out_shape=[jax.ShapeDtypeStruct((t, d), F32)] + cast_shapes,
        scratch_shapes=[pltpu.VMEM((tm, d), BF16)],
        compiler_params=_params(("arbitrary", "arbitrary"), vmem),
        name="ffn",
    )(x2d, mod, wg_arr, wu_arr, w_down, *((head,) if has_head else ()), *cast)
    return outs[0], outs[1:]


def _head_rms_gain(p, gain):
    outs = []
    for hd in range(p.shape[1] // HEAD_DIM):
        ph = p[:, hd * HEAD_DIM:(hd + 1) * HEAD_DIM]
        ms = jnp.mean(ph * ph, axis=-1, keepdims=True)
        outs.append(ph * lax.rsqrt(ms + EPS) * gain)
    return jnp.concatenate(outs, axis=1)


def _in_proj_kernel(x_ref, mod_ref, w_ref, qg_ref, kg_ref, *rest, shift_row, q_scale, n_cast):
    cast_in, o_ref, cast_out, (h_ref,) = _split_cast_refs(rest, n_cast)
    n = pl.program_id(1)
    tm = x_ref.shape[0]

    @pl.when(n == 0)
    def _():
        for r in range(tm // ROW_CHUNK):
            _rms_mod_rows(x_ref, mod_ref, h_ref, shift_row, pl.ds(r * ROW_CHUNK, ROW_CHUNK))
        p = jnp.dot(h_ref[...], w_ref[...], preferred_element_type=F32)
        o_ref[...] = _head_rms_gain(p, qg_ref[...] * q_scale).astype(BF16)
        _side_cast(cast_in, cast_out)

    @pl.when(n == 1)
    def _():
        p = jnp.dot(h_ref[...], w_ref[...], preferred_element_type=F32)
        o_ref[...] = _head_rms_gain(p, kg_ref[...]).astype(BF16)
        _side_cast(cast_in, cast_out)

    @pl.when(n >= 2)
    def _():
        o_ref[...] = jnp.dot(h_ref[...], w_ref[...], preferred_element_type=F32).astype(BF16)
        _side_cast(cast_in, cast_out)


def _in_proj(x2d, mod, w_in, q_gain, k_gain, *, shift_row, seq, cast=(), layer=0):
    t, d = x2d.shape
    n_cols = w_in.shape[1]
    tm = PROJ_TM
    n_i, n_n = t // tm, n_cols // PROJ_TN
    cast_in_specs, cast_out_specs, cast_shapes, cast_vmem = _side_cast_specs(cast, layer, n_i, n_n)
    vmem = (2 * tm * d * 4 + tm * d * 2 + 2 * d * PROJ_TN * 2 + 2 * tm * PROJ_TN * 2 + 6 * tm * PROJ_TN * 4
            + cast_vmem + (4 << 20))
    q_scale = LOG2E / math.sqrt(HEAD_DIM)
    outs = pl.pallas_call(
        functools.partial(_in_proj_kernel, shift_row=shift_row, q_scale=q_scale, n_cast=len(cast)),
        grid=(n_i, n_n),
        in_specs=[pl.BlockSpec((tm, d), lambda i, n: (i, 0)),
                  pl.BlockSpec((1, N_MOD, d), lambda i, n: (i * tm // seq, 0, 0)),
                  pl.BlockSpec((d, PROJ_TN), lambda i, n: (0, n)),
                  pl.BlockSpec((1, HEAD_DIM), lambda i, n: (0, 0)),
                  pl.BlockSpec((1, HEAD_DIM), lambda i, n: (0, 0))] + cast_in_specs,
        out_specs=[pl.BlockSpec((tm, PROJ_TN), lambda i, n: (i, n))] + cast_out_specs,
        out_shape=[jax.ShapeDtypeStruct((t, n_cols), BF16)] + cast_shapes,
        scratch_shapes=[pltpu.VMEM((tm, d), BF16)],
        compiler_params=_params(("arbitrary", "arbitrary"), vmem),
        name="in_proj",
    )(x2d, mod, w_in, q_gain, k_gain, *cast)
    return outs[0], outs[1:]


def _attn_step(chains, neg_tri, fillers=()):
    n_c = len(chains)
    z, sp16, rs, neg_incl, a, pv = ({} for _ in range(6))

    def scores(c):
        q, k = chains[c][0], chains[c][1]
        z[c] = lax.dot_general(q, k, (((1,), (1,)), ((), ())), preferred_element_type=F32)

    def softplus(c):
        mask = chains[c][3]
        sp = jnp.maximum(z[c], 0.0) + jnp.log(1.0 + jnp.exp2(jnp.minimum(z[c], -z[c]))) * LOG2E
        if mask is not None:
            sp = jnp.where(mask, sp, 0.0)
        sp16[c] = sp.astype(BF16)
        rs[c] = jnp.sum(sp, axis=-1, keepdims=True)

    def suffix_sum(c):
        neg_incl[c] = jnp.dot(sp16[c], neg_tri, preferred_element_type=F32)

    def weights(c):
        mask, carry_fn = chains[c][3], chains[c][4]
        w = jnp.exp2(z[c] + neg_incl[c] + carry_fn(rs))
        if mask is not None:
            w = jnp.where(mask, w, 0.0)
        a[c] = w.astype(BF16)

    def values(c):
        pv[c] = jnp.dot(a[c], chains[c][2], preferred_element_type=F32)

    stages = (scores, softplus, suffix_sum, weights, values)
    n_waves = n_c + len(stages) - 1
    done_fill = 0
    for wave in range(n_waves):
        for s in range(len(stages)):
            c = wave - s
            if 0 <= c < n_c:
                stages[s](c)
        while done_fill < (wave + 1) * len(fillers) // n_waves:
            fillers[done_fill]()
            done_fill += 1
    return [pv[c] for c in range(n_c)], [rs[c] for c in range(n_c)]


def _mixer_kernel(qkv_ref, kp_ref, vp_ref, kv_hbm, x_ref, mod_ref, bcu_ref, cuh_ref, cw_ref, w_ref, *rest,
                  gate_row, nq, n_cast):
    cast_in, o_ref, cast_out, (attn_ref, cu_ref, kbuf_ref, vbuf_ref, kv_sem) = _split_cast_refs(rest, n_cast)
    b = pl.program_id(0)
    i = pl.program_id(1)
    t = qkv_ref.shape[0]
    halo = cuh_ref.shape[0]
    width = attn_ref.shape[1]
    n_h = width // HEAD_DIM
    d = o_ref.shape[1]
    q_ref, kd_ref, vd_ref = (qkv_ref.at[:, n * width:(n + 1) * width] for n in range(3))
    gb_ref, gc_ref, u_ref = (bcu_ref.at[:, n * width:(n + 1) * width] for n in range(3))
    gch_ref, uh_ref = (cuh_ref.at[:, n * width:(n + 1) * width] for n in range(2))

    def out_proj_chunks():
        attn_prev = attn_ref[...]
        prev = gch_ref[...].astype(F32) * uh_ref[...].astype(F32)
        cu_ref[0:halo, :] = jnp.where(i == 1, 0.0, prev)
        cu_ref[halo:halo + t, :] = gc_ref[...].astype(F32) * u_ref[...].astype(F32)
        conv = cu_ref[halo:halo + t, :] * cw_ref[CONV_K - 1:CONV_K, :]
        for k in range(CONV_K - 1):
            back = CONV_K - 1 - k
            conv = conv + cu_ref[halo - back:halo - back + t, :] * cw_ref[k:k + 1, :]
        y = (gb_ref[...].astype(F32) * conv).astype(BF16)
        gate = mod_ref[0, gate_row:gate_row + 1, :]

        def out_chunk(c):
            cols = slice(c * OUT_TN, (c + 1) * OUT_TN)
            mix = jnp.dot(attn_prev, w_ref[0:width, cols], preferred_element_type=F32)
            mix = mix + jnp.dot(y, w_ref[width:, cols], preferred_element_type=F32)
            o_ref[:, cols] = x_ref[:, cols] + gate[:, cols] * mix

        return [functools.partial(out_chunk, c) for c in range(d // OUT_TN)]

    def attention(fillers, has_prev):
        row = lax.broadcasted_iota(jnp.int32, (t, t), 0)
        col = lax.broadcasted_iota(jnp.int32, (t, t), 1)
        neg_tri = jnp.where(row >= col, -1.0, 0.0).astype(BF16)
        causal = col < row

        def lanes(hd):
            return slice(hd * HEAD_DIM, (hd + 1) * HEAD_DIM)

        def heads_of(kblk_ref, vblk_ref):
            return [(kblk_ref[:, lanes(hd)], vblk_ref[:, lanes(hd)]) for hd in range(n_h)]

        def fetch_key_block(j):
            rows = pl.ds(pl.multiple_of((b * nq + j) * t, t), t)
            copies = [pltpu.make_async_copy(kv_hbm.at[rows, pl.ds(col * width, width)], buf, kv_sem.at[n])
                      for n, (col, buf) in enumerate(((K_COL, kbuf_ref), (V_COL, vbuf_ref)))]
            for cp in copies:
                cp.start()
            for cp in copies:
                cp.wait()

        qs = [q_ref[:, lanes(hd)] for hd in range(n_h)]
        zero = jnp.zeros((t, 1), F32)
        kv = heads_of(kd_ref, vd_ref)
        chains = [(qs[hd], kv[hd][0], kv[hd][1], causal, lambda rs: zero) for hd in range(n_h)]
        if has_prev:
            kv = heads_of(kp_ref, vp_ref)
            chains += [(qs[hd], kv[hd][0], kv[hd][1], None, lambda rs, hd=hd: -rs[hd]) for hd in range(n_h)]
        pvs, rss = _attn_step(chains, neg_tri, fillers)
        if has_prev:
            accs = tuple(pvs[hd] + pvs[n_h + hd] for hd in range(n_h))
            carries = tuple(-rss[hd] - rss[n_h + hd] for hd in range(n_h))

            def live_max(carries):
                live = carries[0]
                for cr in carries[1:]:
                    live = jnp.maximum(live, cr)
                return jnp.max(live)

            def cond(state):
                jj, live = state[0], state[1]
                return jnp.logical_and(jj < i, live > F32_EXP2_UNDERFLOW)

            def body(state):
                jj, _, accs, carries = state
                fetch_key_block(i - 1 - jj)
                kv = heads_of(kbuf_ref, vbuf_ref)
                pvs, rss = _attn_step([(qs[hd], kv[hd][0], kv[hd][1], None, lambda rs, hd=hd: carries[hd])
                                       for hd in range(n_h)], neg_tri)
                accs = tuple(acc + pv for acc, pv in zip(accs, pvs))
                carries = tuple(cr - rs for cr, rs in zip(carries, rss))
                return jj + 1, live_max(carries), accs, carries

            _, _, accs, _ = lax.while_loop(cond, body, (jnp.int32(1), live_max(carries), accs, carries))
        else:
            accs = pvs
        for hd in range(n_h):
            attn_ref[:, lanes(hd)] = accs[hd].astype(attn_ref.dtype)

    @pl.when(i == 0)
    def _():
        attention((), has_prev=False)
        _side_cast(cast_in, cast_out)

    @pl.when(jnp.logical_and(i > 0, i < nq))
    def _():
        attention(out_proj_chunks(), has_prev=True)
        _side_cast(cast_in, cast_out)

    @pl.when(i == nq)
    def _():
        for job in out_proj_chunks():
            job()
        _side_cast(cast_in, cast_out)


def _mixer(x2d, mod, proj, conv_w, w_out, *, gate_row, bsz, seq, n_heads, cast=(), layer=0):
    t_all, d = x2d.shape
    width = n_heads * HEAD_DIM
    t = ATT_T
    nq = seq // t
    halo = BF16_SUBLANE_TILE
    per = t // halo

    def q_row(b, i):
        return b * nq + jnp.minimum(i, nq - 1)

    def prev_row(b, i):
        return b * nq + jnp.maximum(jnp.minimum(i, nq - 1) - 1, 0)

    def out_row(b, i):
        return b * nq + jnp.maximum(i - 1, 0)

    cast_in_specs, cast_out_specs, cast_shapes, cast_vmem = _side_cast_specs(cast, layer, bsz, nq + 1)
    vmem = (2 * 5 * t * width * 2 + 2 * t * width * 2 + d * d * 2 + 2 * 2 * t * d * 4 + 2 * 4 * t * width * 2
            + (t + halo) * width * 4 + 2 * n_heads * 12 * t * t * 4 + cast_vmem + (4 << 20))
    outs = pl.pallas_call(
        functools.partial(_mixer_kernel, gate_row=gate_row, nq=nq, n_cast=len(cast)),
        grid=(bsz, nq + 1),
        in_specs=[pl.BlockSpec((t, 3 * width), lambda b, i: (q_row(b, i), 0)),
                  pl.BlockSpec((t, width), lambda b, i: (prev_row(b, i), K_COL)),
                  pl.BlockSpec((t, width), lambda b, i: (prev_row(b, i), V_COL)),
                  pl.BlockSpec(memory_space=pl.ANY),
                  pl.BlockSpec((t, d), lambda b, i: (out_row(b, i), 0)),
                  pl.BlockSpec((1, N_MOD, d), lambda b, i: (b, 0, 0)),
                  pl.BlockSpec((t, 3 * width), lambda b, i: (out_row(b, i), 1)),
                  pl.BlockSpec((halo, 2 * width), lambda b, i: (jnp.maximum(out_row(b, i) * per - 1, 0), 2)),
                  pl.BlockSpec((CONV_K, width), lambda b, i: (0, 0)),
                  pl.BlockSpec((d, d), lambda b, i: (0, 0))] + cast_in_specs,
        out_specs=[pl.BlockSpec((t, d), lambda b, i: (out_row(b, i), 0))] + cast_out_specs,
        out_shape=[jax.ShapeDtypeStruct((t_all, d), F32)] + cast_shapes,
        scratch_shapes=[pltpu.VMEM((t, width), BF16), pltpu.VMEM((t + halo, width), F32),
                        pltpu.VMEM((t, width), BF16), pltpu.VMEM((t, width), BF16),
                        pltpu.SemaphoreType.DMA((2,))],
        compiler_params=_params(("arbitrary", "arbitrary"), vmem),
        name="mixer",
    )(proj, proj, proj, proj, x2d, mod, proj, proj, conv_w, w_out, *cast)
    return outs[0], outs[1:]


def kernel(x, c, w_ada, b_ada, w1_gu, w1_down, w_in, q_norm_w, k_norm_w, conv_w, w_out, w2_gu, w2_down):
    bsz, seq, d = x.shape
    depth = w_ada.shape[0]
    width = conv_w.shape[2]
    n_heads = width // HEAD_DIM
    assert w_in.shape[2] == 6 * width and width == PROJ_TN
    assert seq % max(FFN_TM, PROJ_TM, ATT_T) == 0 and d % OUT_TN == 0

    x2d = x.reshape(bsz * seq, d)
    for l in range(depth):
        mod = _adaln(c, w_ada[l], b_ada[l]).reshape(bsz, N_MOD, d)
        head, w1_g_b, w1_u_b, w1_down_b = _ffn_head(x2d, mod, w1_gu, w1_down, shift_row=0, layer=l)
        x2d, (w_in_b, w_out_b) = _ffn(
            x2d, mod, (w1_g_b, 0), (w1_u_b, 0), w1_down_b, shift_row=0, seq=seq, head=head,
            cast=(w_in, w_out), layer=l)
        proj, (w2_gu_b,) = _in_proj(
            x2d, mod, w_in_b, q_norm_w[l].reshape(1, HEAD_DIM), k_norm_w[l].reshape(1, HEAD_DIM),
            shift_row=3, seq=seq, cast=(w2_gu,), layer=l)
        x2d, (w2_down_b,) = _mixer(x2d, mod, proj, conv_w[l], w_out_b, gate_row=5, bsz=bsz, seq=seq,
                                   n_heads=n_heads, cast=(w2_down,), layer=l)
        n_f = w2_down_b.shape[0] // FFN_TF
        x2d, _ = _ffn(x2d, mod, (w2_gu_b, 0), (w2_gu_b, n_f), w2_down_b, shift_row=6, seq=seq)
    return x2d.reshape(bsz, seq, d)
```

```python
import functools
import math

import jax
import jax.numpy as jnp
from jax import lax
from jax.experimental import pallas as pl
from jax.experimental.pallas import tpu as pltpu

F32 = jnp.float32
BF16 = jnp.bfloat16

HEAD_DIM = 128
N_NORMED_BLOCKS = 2
K_COL, V_COL = 1, 2
CONV_K = 3
N_MOD = 9
FFN_RES = 0.5
EPS = 1e-6
LOG2E = 1.4426950408889634

F32_EXP2_UNDERFLOW = -150.0

V7X_VMEM_LIMIT_BYTES = 58 * 1024 * 1024
BF16_SUBLANE_TILE = 16

ADA_TN = 2048
FFN_TM = 1024
FFN_TF = 512
FFN_HEAD_TF = 256
PROJ_TM = 1024
PROJ_TN = 1024
ATT_T = 256
OUT_TN = 256
ROW_CHUNK = 256


def _params(semantics, vmem_bytes):
    return pltpu.CompilerParams(dimension_semantics=semantics,
                                vmem_limit_bytes=min(int(vmem_bytes), V7X_VMEM_LIMIT_BYTES))


def _rms_mod_rows(x_ref, mod_ref, h_ref, shift_row, rows):
    xv = x_ref[rows, :]
    ms = jnp.mean(xv * xv, axis=-1, keepdims=True)
    hn = xv * lax.rsqrt(ms + EPS)
    shift = mod_ref[0, shift_row:shift_row + 1, :]
    scale = mod_ref[0, shift_row + 1:shift_row + 2, :]
    h_ref[rows, :] = (hn * (1.0 + scale) + shift).astype(BF16)


def _adaln_kernel(c_ref, w_ref, b_ref, o_ref):
    cv = c_ref[...]
    c_act = (cv * jax.nn.sigmoid(cv)).astype(BF16)
    o_ref[...] = jnp.dot(c_act, w_ref[...].astype(BF16), preferred_element_type=F32) + b_ref[...]


def _adaln(c, w_ada, b_ada):
    bsz, d = c.shape
    n = w_ada.shape[1]
    return pl.pallas_call(
        _adaln_kernel,
        grid=(n // ADA_TN,),
        in_specs=[pl.BlockSpec((bsz, d), lambda j: (0, 0)),
                  pl.BlockSpec((d, ADA_TN), lambda j: (0, j)),
                  pl.BlockSpec((1, ADA_TN), lambda j: (0, j))],
        out_specs=pl.BlockSpec((bsz, ADA_TN), lambda j: (0, j)),
        out_shape=jax.ShapeDtypeStruct((bsz, n), F32),
        compiler_params=_params(("arbitrary",), 4 * d * ADA_TN * 4),
        name="adaln",
    )(c, w_ada, b_ada.reshape(1, n))


def _cast_blocking(rows, cols, n_i, n_j):
    lane_groups = cols // 128
    n_c = max(k for k in range(1, n_j + 1) if lane_groups % k == 0)
    assert rows % (n_i * BF16_SUBLANE_TILE) == 0 and cols % 128 == 0
    return rows // n_i, cols // n_c, n_c


def _side_cast_specs(cast, layer, n_i, n_j, start=0):
    blockings = [_cast_blocking(w.shape[1], w.shape[2], n_i, n_j - start) for w in cast]

    def cast_map(n_c, lead):
        return lambda i, j: lead + (i, jnp.clip(j - start, 0, n_c - 1))

    in_specs = [pl.BlockSpec((None, r, c), cast_map(n_c, (layer,))) for r, c, n_c in blockings]
    out_specs = [pl.BlockSpec((r, c), cast_map(n_c, ())) for r, c, n_c in blockings]
    out_shapes = [jax.ShapeDtypeStruct(w.shape[1:], BF16) for w in cast]
    return in_specs, out_specs, out_shapes, sum(2 * r * c * 6 for r, c, _ in blockings)


def _split_cast_refs(rest, n_cast):
    return rest[:n_cast], rest[n_cast], rest[n_cast + 1:2 * n_cast + 1], rest[2 * n_cast + 1:]


def _side_cast(cast_in, cast_out):
    for src_ref, dst_ref in zip(cast_in, cast_out):
        dst_ref[...] = src_ref[...].astype(BF16)


def _ffn_step(first, last, x_ref, mod_ref, h_ref, o_ref, weights, shift_row):
    if first:
        for r in range(x_ref.shape[0] // ROW_CHUNK):
            _rms_mod_rows(x_ref, mod_ref, h_ref, shift_row, pl.ds(r * ROW_CHUNK, ROW_CHUNK))
    wg, wu, wd = weights()
    h = h_ref[...]
    g = jnp.dot(h, wg, preferred_element_type=F32)
    u = jnp.dot(h, wu, preferred_element_type=F32)
    act = (g * jax.nn.sigmoid(g) * u).astype(BF16)
    part = jnp.dot(act, wd, preferred_element_type=F32)
    if first:
        o_ref[...] = part
    elif last:
        gate = FFN_RES * mod_ref[0, shift_row + 2:shift_row + 3, :]
        o_ref[...] = x_ref[...] + gate * (o_ref[...] + part)
    else:
        o_ref[...] += part


def _ffn_variants(active, f, n_f, step):
    def when(cond):
        return pl.when(cond if active is None else jnp.logical_and(active, cond))
    when(f == 0)(functools.partial(step, True, False))
    when(jnp.logical_and(f > 0, f < n_f - 1))(functools.partial(step, False, False))
    when(f == n_f - 1)(functools.partial(step, False, True))


def _ffn_head_kernel(x_ref, mod_ref, wg32_ref, wu32_ref, wd32_ref, o_ref, wg16_ref, wu16_ref, wd16_ref,
                     h_ref, *, shift_row, n_f):
    def weights():
        blocks = []
        for src_ref, dst_ref in ((wg32_ref, wg16_ref), (wu32_ref, wu16_ref), (wd32_ref, wd16_ref)):
            w = src_ref[...].astype(BF16)
            dst_ref[...] = w
            blocks.append(w)
        return blocks

    def step(first, last):
        _ffn_step(first, last, x_ref, mod_ref, h_ref, o_ref, weights, shift_row)

    _ffn_variants(None, pl.program_id(1), n_f, step)


def _ffn_head(x2d, mod, w_gu, w_down, *, shift_row, layer):
    d = x2d.shape[1]
    d_ff = w_down.shape[1]
    tm, tf = FFN_TM, FFN_HEAD_TF
    n_f = d_ff // tf
    assert n_f >= 3
    once = pl.Buffered(1)
    vmem = 2 * tm * d * 4 + tm * d * 2 + 2 * 3 * d * tf * 6 + 6 * tm * tf * 4 + (4 << 20)
    return pl.pallas_call(
        functools.partial(_ffn_head_kernel, shift_row=shift_row, n_f=n_f),
        grid=(1, n_f),
        in_specs=[pl.BlockSpec((tm, d), lambda i, f: (0, 0), pipeline_mode=once),
                  pl.BlockSpec((1, N_MOD, d), lambda i, f: (0, 0, 0)),
                  pl.BlockSpec((None, d, tf), lambda i, f: (layer, 0, f)),
                  pl.BlockSpec((None, d, tf), lambda i, f: (layer, 0, f + n_f)),
                  pl.BlockSpec((None, tf, d), lambda i, f: (layer, f, 0))],
        out_specs=[pl.BlockSpec((tm, d), lambda i, f: (0, 0)),
                   pl.BlockSpec((d, tf), lambda i, f: (0, f)),
                   pl.BlockSpec((d, tf), lambda i, f: (0, f)),
                   pl.BlockSpec((tf, d), lambda i, f: (f, 0))],
        out_shape=[jax.ShapeDtypeStruct((tm, d), F32),
                   jax.ShapeDtypeStruct((d, d_ff), BF16),
                   jax.ShapeDtypeStruct((d, d_ff), BF16),
                   jax.ShapeDtypeStruct((d_ff, d), BF16)],
        scratch_shapes=[pltpu.VMEM((tm, d), BF16)],
        compiler_params=_params(("arbitrary", "arbitrary"), vmem),
        name="ffn_head",
    )(x2d, mod, w_gu, w_gu, w_down)


def _ffn_kernel(x_ref, mod_ref, wg_ref, wu_ref, wd_ref, *rest, shift_row, n_f, n_cast, has_head):
    head_ref, rest = (rest[0], rest[1:]) if has_head else (None, rest)
    cast_in, o_ref, cast_out, (h_ref,) = _split_cast_refs(rest, n_cast)
    i = pl.program_id(0)
    f = pl.program_id(1)

    def step(first, last):
        _ffn_step(first, last, x_ref, mod_ref, h_ref, o_ref,
                  lambda: (wg_ref[...], wu_ref[...], wd_ref[...]), shift_row)
        _side_cast(cast_in, cast_out)

    _ffn_variants(i > 0 if has_head else None, f, n_f, step)

    if has_head:
        @pl.when(i == 0)
        def _():
            _side_cast(cast_in, cast_out)

        @pl.when(jnp.logical_and(i == 0, f == n_f - 1))
        def _():
            pltpu.sync_copy(head_ref, o_ref)


def _ffn(x2d, mod, w_g, w_u, w_down, *, shift_row, seq, head=None, cast=(), layer=0):
    t, d = x2d.shape
    d_ff = w_down.shape[0]
    n_f = d_ff // FFN_TF
    assert n_f >= 3
    tm = FFN_TM
    n_i = t // tm
    has_head = head is not None
    (wg_arr, g_off), (wu_arr, u_off) = w_g, w_u

    def col(f, i):
        return jnp.where(i == 0, 0, f) if has_head else f

    cast_in_specs, cast_out_specs, cast_shapes, cast_vmem = _side_cast_specs(cast, layer, n_i, n_f)
    vmem = ((2 * tm * d * 4) * 2 + tm * d * 2 + 2 * 3 * d * FFN_TF * 2 + 6 * tm * FFN_TF * 4
            + cast_vmem + (4 << 20))
    outs = pl.pallas_call(
        functools.partial(_ffn_kernel, shift_row=shift_row, n_f=n_f, n_cast=len(cast), has_head=has_head),
        grid=(n_i, n_f),
        in_specs=[pl.BlockSpec((tm, d), lambda i, f: (jnp.maximum(i, 1) if has_head else i, 0)),
                  pl.BlockSpec((1, N_MOD, d), lambda i, f: (i * tm // seq, 0, 0)),
                  pl.BlockSpec((d, FFN_TF), lambda i, f: (0, g_off + col(f, i))),
                  pl.BlockSpec((d, FFN_TF), lambda i, f: (0, u_off + col(f, i))),
                  pl.BlockSpec((FFN_TF, d), lambda i, f: (col(f, i), 0))]
                 + ([pl.BlockSpec(memory_space=pl.ANY)] if has_head else []) + cast_in_specs,
        out_specs=[pl.BlockSpec((tm, d), lambda i, f: (i, 0))] + cast_out_specs,
        out_shape=[jax.ShapeDtypeStruct((t, d), F32)] + cast_shapes,
        scratch_shapes=[pltpu.VMEM((tm, d), BF16)],
        compiler_params=_params(("arbitrary", "arbitrary"), vmem),
        name="ffn",
    )(x2d, mod, wg_arr, wu_arr, w_down, *((head,) if has_head else ()), *cast)
    return outs[0], outs[1:]


def _head_rms_gain(p, gain):
    outs = []
    for hd in range(p.shape[1] // HEAD_DIM):
        ph = p[:, hd * HEAD_DIM:(hd + 1) * HEAD_DIM]
        ms = jnp.mean(ph * ph, axis=-1, keepdims=True)
        outs.append(ph * lax.rsqrt(ms + EPS) * gain)
    return jnp.concatenate(outs, axis=1)


def _in_proj_kernel(x_ref, mod_ref, w_ref, qg_ref, kg_ref, *rest, shift_row, q_scale, n_cast):
    cast_in, o_ref, cast_out, (h_ref,) = _split_cast_refs(rest, n_cast)
    n = pl.program_id(1)
    tm = x_ref.shape[0]

    @pl.when(n == 0)
    def _():
        for r in range(tm // ROW_CHUNK):
            _rms_mod_rows(x_ref, mod_ref, h_ref, shift_row, pl.ds(r * ROW_CHUNK, ROW_CHUNK))
        p = jnp.dot(h_ref[...], w_ref[...], preferred_element_type=F32)
        o_ref[...] = _head_rms_gain(p, qg_ref[...] * q_scale).astype(BF16)

    @pl.when(n == 1)
    def _():
        p = jnp.dot(h_ref[...], w_ref[...], preferred_element_type=F32)
        o_ref[...] = _head_rms_gain(p, kg_ref[...]).astype(BF16)

    @pl.when(n >= N_NORMED_BLOCKS)
    def _():
        o_ref[...] = jnp.dot(h_ref[...], w_ref[...], preferred_element_type=F32).astype(BF16)
        _side_cast(cast_in, cast_out)


def _in_proj(x2d, mod, w_in, q_gain, k_gain, *, shift_row, seq, cast=(), layer=0):
    t, d = x2d.shape
    n_cols = w_in.shape[1]
    tm = PROJ_TM
    n_i, n_n = t // tm, n_cols // PROJ_TN
    cast_in_specs, cast_out_specs, cast_shapes, cast_vmem = _side_cast_specs(
        cast, layer, n_i, n_n, start=N_NORMED_BLOCKS)
    vmem = (2 * tm * d * 4 + tm * d * 2 + 2 * d * PROJ_TN * 2 + 2 * tm * PROJ_TN * 2 + 6 * tm * PROJ_TN * 4
            + cast_vmem + (4 << 20))
    q_scale = LOG2E / math.sqrt(HEAD_DIM)
    outs = pl.pallas_call(
        functools.partial(_in_proj_kernel, shift_row=shift_row, q_scale=q_scale, n_cast=len(cast)),
        grid=(n_i, n_n),
        in_specs=[pl.BlockSpec((tm, d), lambda i, n: (i, 0)),
                  pl.BlockSpec((1, N_MOD, d), lambda i, n: (i * tm // seq, 0, 0)),
                  pl.BlockSpec((d, PROJ_TN), lambda i, n: (0, n)),
                  pl.BlockSpec((1, HEAD_DIM), lambda i, n: (0, 0)),
                  pl.BlockSpec((1, HEAD_DIM), lambda i, n: (0, 0))] + cast_in_specs,
        out_specs=[pl.BlockSpec((tm, PROJ_TN), lambda i, n: (i, n))] + cast_out_specs,
        out_shape=[jax.ShapeDtypeStruct((t, n_cols), BF16)] + cast_shapes,
        scratch_shapes=[pltpu.VMEM((tm, d), BF16)],
        compiler_params=_params(("arbitrary", "arbitrary"), vmem),
        name="in_proj",
    )(x2d, mod, w_in, q_gain, k_gain, *cast)
    return outs[0], outs[1:]


def _attn_step(chains, neg_tri, fillers=()):
    n_c = len(chains)
    z, sp16, rs, neg_incl, a, pv = ({} for _ in range(6))

    def scores(c):
        q, k = chains[c][0], chains[c][1]
        z[c] = lax.dot_general(q, k, (((1,), (1,)), ((), ())), preferred_element_type=F32)

    def softplus(c):
        mask = chains[c][3]
        sp = jnp.maximum(z[c], 0.0) + jnp.log(1.0 + jnp.exp2(jnp.minimum(z[c], -z[c]))) * LOG2E
        if mask is not None:
            sp = jnp.where(mask, sp, 0.0)
        sp16[c] = sp.astype(BF16)
        rs[c] = jnp.sum(sp, axis=-1, keepdims=True)

    def suffix_sum(c):
        neg_incl[c] = jnp.dot(sp16[c], neg_tri, preferred_element_type=F32)

    def weights(c):
        mask, carry_fn = chains[c][3], chains[c][4]
        w = jnp.exp2(z[c] + neg_incl[c] + carry_fn(rs))
        if mask is not None:
            w = jnp.where(mask, w, 0.0)
        a[c] = w.astype(BF16)

    def values(c):
        pv[c] = jnp.dot(a[c], chains[c][2], preferred_element_type=F32)

    stages = (scores, softplus, suffix_sum, weights, values)
    n_waves = n_c + len(stages) - 1
    done_fill = 0
    for wave in range(n_waves):
        for s in range(len(stages)):
            c = wave - s
            if 0 <= c < n_c:
                stages[s](c)
        while done_fill < (wave + 1) * len(fillers) // n_waves:
            fillers[done_fill]()
            done_fill += 1
    return [pv[c] for c in range(n_c)], [rs[c] for c in range(n_c)]


def _mixer_kernel(qkv_ref, kp_ref, vp_ref, kv_hbm, x_ref, mod_ref, bcu_ref, cuh_ref, cw_ref, w_ref, *rest,
                  gate_row, nq, n_cast):
    cast_in, o_ref, cast_out, (attn_ref, cu_ref, kbuf_ref, vbuf_ref, kv_sem) = _split_cast_refs(rest, n_cast)
    b = pl.program_id(0)
    i = pl.program_id(1)
    t = qkv_ref.shape[0]
    halo = cuh_ref.shape[0]
    width = attn_ref.shape[1]
    n_h = width // HEAD_DIM
    d = o_ref.shape[1]
    q_ref, kd_ref, vd_ref = (qkv_ref.at[:, n * width:(n + 1) * width] for n in range(3))
    gb_ref, gc_ref, u_ref = (bcu_ref.at[:, n * width:(n + 1) * width] for n in range(3))
    gch_ref, uh_ref = (cuh_ref.at[:, n * width:(n + 1) * width] for n in range(2))

    def out_proj_chunks():
        attn_prev = attn_ref[...]
        prev = gch_ref[...].astype(F32) * uh_ref[...].astype(F32)
        cu_ref[0:halo, :] = jnp.where(i == 1, 0.0, prev)
        cu_ref[halo:halo + t, :] = gc_ref[...].astype(F32) * u_ref[...].astype(F32)
        conv = cu_ref[halo:halo + t, :] * cw_ref[CONV_K - 1:CONV_K, :]
        for k in range(CONV_K - 1):
            back = CONV_K - 1 - k
            conv = conv + cu_ref[halo - back:halo - back + t, :] * cw_ref[k:k + 1, :]
        y = (gb_ref[...].astype(F32) * conv).astype(BF16)
        gate = mod_ref[0, gate_row:gate_row + 1, :]

        def out_chunk(c):
            cols = slice(c * OUT_TN, (c + 1) * OUT_TN)
            mix = jnp.dot(attn_prev, w_ref[0:width, cols], preferred_element_type=F32)
            mix = mix + jnp.dot(y, w_ref[width:, cols], preferred_element_type=F32)
            o_ref[:, cols] = x_ref[:, cols] + gate[:, cols] * mix

        return [functools.partial(out_chunk, c) for c in range(d // OUT_TN)]

    def attention(fillers, has_prev):
        row = lax.broadcasted_iota(jnp.int32, (t, t), 0)
        col = lax.broadcasted_iota(jnp.int32, (t, t), 1)
        neg_tri = jnp.where(row >= col, -1.0, 0.0).astype(BF16)
        causal = col < row

        def lanes(hd):
            return slice(hd * HEAD_DIM, (hd + 1) * HEAD_DIM)

        def heads_of(kblk_ref, vblk_ref):
            return [(kblk_ref[:, lanes(hd)], vblk_ref[:, lanes(hd)]) for hd in range(n_h)]

        def fetch_key_block(j):
            rows = pl.ds(pl.multiple_of((b * nq + j) * t, t), t)
            copies = [pltpu.make_async_copy(kv_hbm.at[rows, pl.ds(col * width, width)], buf, kv_sem.at[n])
                      for n, (col, buf) in enumerate(((K_COL, kbuf_ref), (V_COL, vbuf_ref)))]
            for cp in copies:
                cp.start()
            for cp in copies:
                cp.wait()

        qs = [q_ref[:, lanes(hd)] for hd in range(n_h)]
        zero = jnp.zeros((t, 1), F32)
        kv = heads_of(kd_ref, vd_ref)
        chains = [(qs[hd], kv[hd][0], kv[hd][1], causal, lambda rs: zero) for hd in range(n_h)]
        if has_prev:
            kv = heads_of(kp_ref, vp_ref)
            chains += [(qs[hd], kv[hd][0], kv[hd][1], None, lambda rs, hd=hd: -rs[hd]) for hd in range(n_h)]
        pvs, rss = _attn_step(chains, neg_tri, fillers)
        if has_prev:
            accs = tuple(pvs[hd] + pvs[n_h + hd] for hd in range(n_h))
            carries = tuple(-rss[hd] - rss[n_h + hd] for hd in range(n_h))

            def live_max(carries):
                live = carries[0]
                for cr in carries[1:]:
                    live = jnp.maximum(live, cr)
                return jnp.max(live)

            def cond(state):
                jj, live = state[0], state[1]
                return jnp.logical_and(jj < i, live > F32_EXP2_UNDERFLOW)

            def body(state):
                jj, _, accs, carries = state
                fetch_key_block(i - 1 - jj)
                kv = heads_of(kbuf_ref, vbuf_ref)
                pvs, rss = _attn_step([(qs[hd], kv[hd][0], kv[hd][1], None, lambda rs, hd=hd: carries[hd])
                                       for hd in range(n_h)], neg_tri)
                accs = tuple(acc + pv for acc, pv in zip(accs, pvs))
                carries = tuple(cr - rs for cr, rs in zip(carries, rss))
                return jj + 1, live_max(carries), accs, carries

            _, _, accs, _ = lax.while_loop(cond, body, (jnp.int32(1), live_max(carries), accs, carries))
        else:
            accs = pvs
        for hd in range(n_h):
            attn_ref[:, lanes(hd)] = accs[hd].astype(attn_ref.dtype)

    @pl.when(i == 0)
    def _():
        attention((), has_prev=False)
        _side_cast(cast_in, cast_out)

    @pl.when(jnp.logical_and(i > 0, i < nq))
    def _():
        attention(out_proj_chunks(), has_prev=True)
        _side_cast(cast_in, cast_out)

    @pl.when(i == nq)
    def _():
        for job in out_proj_chunks():
            job()
        _side_cast(cast_in, cast_out)


def _mixer(x2d, mod, proj, conv_w, w_out, *, gate_row, bsz, seq, n_heads, cast=(), layer=0):
    t_all, d = x2d.shape
    width = n_heads * HEAD_DIM
    t = ATT_T
    nq = seq // t
    halo = BF16_SUBLANE_TILE
    per = t // halo

    def q_row(b, i):
        return b * nq + jnp.minimum(i, nq - 1)

    def prev_row(b, i):
        return b * nq + jnp.maximum(jnp.minimum(i, nq - 1) - 1, 0)

    def out_row(b, i):
        return b * nq + jnp.maximum(i - 1, 0)

    cast_in_specs, cast_out_specs, cast_shapes, cast_vmem = _side_cast_specs(cast, layer, bsz, nq + 1)
    vmem = (2 * 5 * t * width * 2 + 2 * t * width * 2 + d * d * 2 + 2 * 2 * t * d * 4 + 2 * 4 * t * width * 2
            + (t + halo) * width * 4 + 2 * n_heads * 12 * t * t * 4 + cast_vmem + (4 << 20))
    outs = pl.pallas_call(
        functools.partial(_mixer_kernel, gate_row=gate_row, nq=nq, n_cast=len(cast)),
        grid=(bsz, nq + 1),
        in_specs=[pl.BlockSpec((t, 3 * width), lambda b, i: (q_row(b, i), 0)),
                  pl.BlockSpec((t, width), lambda b, i: (prev_row(b, i), K_COL)),
                  pl.BlockSpec((t, width), lambda b, i: (prev_row(b, i), V_COL)),
                  pl.BlockSpec(memory_space=pl.ANY),
                  pl.BlockSpec((t, d), lambda b, i: (out_row(b, i), 0)),
                  pl.BlockSpec((1, N_MOD, d), lambda b, i: (b, 0, 0)),
                  pl.BlockSpec((t, 3 * width), lambda b, i: (out_row(b, i), 1)),
                  pl.BlockSpec((halo, 2 * width), lambda b, i: (jnp.maximum(out_row(b, i) * per - 1, 0), 2)),
                  pl.BlockSpec((CONV_K, width), lambda b, i: (0, 0)),
                  pl.BlockSpec((d, d), lambda b, i: (0, 0))] + cast_in_specs,
        out_specs=[pl.BlockSpec((t, d), lambda b, i: (out_row(b, i), 0))] + cast_out_specs,
        out_shape=[jax.ShapeDtypeStruct((t_all, d), F32)] + cast_shapes,
        scratch_shapes=[pltpu.VMEM((t, width), BF16), pltpu.VMEM((t + halo, width), F32),
                        pltpu.VMEM((t, width), BF16), pltpu.VMEM((t, width), BF16),
                        pltpu.SemaphoreType.DMA((2,))],
        compiler_params=_params(("arbitrary", "arbitrary"), vmem),
        name="mixer",
    )(proj, proj, proj, proj, x2d, mod, proj, proj, conv_w, w_out, *cast)
    return outs[0], outs[1:]


def kernel(x, c, w_ada, b_ada, w1_gu, w1_down, w_in, q_norm_w, k_norm_w, conv_w, w_out, w2_gu, w2_down):
    bsz, seq, d = x.shape
    depth = w_ada.shape[0]
    width = conv_w.shape[2]
    n_heads = width // HEAD_DIM
    assert w_in.shape[2] == 6 * width and width == PROJ_TN
    assert seq % max(FFN_TM, PROJ_TM, ATT_T) == 0 and d % OUT_TN == 0

    x2d = x.reshape(bsz * seq, d)
    for l in range(depth):
        mod = _adaln(c, w_ada[l], b_ada[l]).reshape(bsz, N_MOD, d)
        head, w1_g_b, w1_u_b, w1_down_b = _ffn_head(x2d, mod, w1_gu, w1_down, shift_row=0, layer=l)
        x2d, (w_in_b, w_out_b) = _ffn(
            x2d, mod, (w1_g_b, 0), (w1_u_b, 0), w1_down_b, shift_row=0, seq=seq, head=head,
            cast=(w_in, w_out), layer=l)
        proj, (w2_gu_b,) = _in_proj(
            x2d, mod, w_in_b, q_norm_w[l].reshape(1, HEAD_DIM), k_norm_w[l].reshape(1, HEAD_DIM),
            shift_row=3, seq=seq, cast=(w2_gu,), layer=l)
        x2d, (w2_down_b,) = _mixer(x2d, mod, proj, conv_w[l], w_out_b, gate_row=5, bsz=bsz, seq=seq,
                                   n_heads=n_heads, cast=(w2_down,), layer=l)
        n_f = w2_down_b.shape[0] // FFN_TF
        x2d, _ = _ffn(x2d, mod, (w2_gu_b, 0), (w2_gu_b, n_f), w2_down_b, shift_row=6, seq=seq)
    return x2d.reshape(bsz, seq, d)
```

```python
import functools
import math

import jax
import jax.numpy as jnp
from jax import lax
from jax.experimental import pallas as pl
from jax.experimental.pallas import tpu as pltpu

F32 = jnp.float32
BF16 = jnp.bfloat16

HEAD_DIM = 128
K_COL, V_COL = 1, 2
CONV_K = 3
N_MOD = 9
FFN_RES = 0.5
EPS = 1e-6
LOG2E = 1.4426950408889634

F32_EXP2_UNDERFLOW = -150.0

V7X_VMEM_LIMIT_BYTES = 58 * 1024 * 1024
BF16_SUBLANE_TILE = 16

ADA_TN = 512
FFN_TM = 1024
FFN_TF = 512
FFN_HEAD_TF = 256
PROJ_TM = 1024
PROJ_TN = 1024
ATT_T = 256
OUT_TN = 256
ROW_CHUNK = 256


def _params(semantics, vmem_bytes):
    return pltpu.CompilerParams(dimension_semantics=semantics,
                                vmem_limit_bytes=min(int(vmem_bytes), V7X_VMEM_LIMIT_BYTES))


def _rms_mod_rows(x_ref, mod_ref, h_ref, shift_row, rows):
    xv = x_ref[rows, :]
    ms = jnp.mean(xv * xv, axis=-1, keepdims=True)
    hn = xv * lax.rsqrt(ms + EPS)
    shift = mod_ref[0, shift_row:shift_row + 1, :]
    scale = mod_ref[0, shift_row + 1:shift_row + 2, :]
    h_ref[rows, :] = (hn * (1.0 + scale) + shift).astype(BF16)


def _adaln_kernel(c_ref, w_ref, b_ref, o_ref):
    cv = c_ref[...]
    c_act = (cv * jax.nn.sigmoid(cv)).astype(BF16)
    o_ref[...] = jnp.dot(c_act, w_ref[...].astype(BF16), preferred_element_type=F32) + b_ref[...]


def _adaln(c, w_ada, b_ada):
    bsz, d = c.shape
    n = w_ada.shape[1]
    return pl.pallas_call(
        _adaln_kernel,
        grid=(n // ADA_TN,),
        in_specs=[pl.BlockSpec((bsz, d), lambda j: (0, 0)),
                  pl.BlockSpec((d, ADA_TN), lambda j: (0, j)),
                  pl.BlockSpec((1, ADA_TN), lambda j: (0, j))],
        out_specs=pl.BlockSpec((bsz, ADA_TN), lambda j: (0, j)),
        out_shape=jax.ShapeDtypeStruct((bsz, n), F32),
        compiler_params=_params(("arbitrary",), 4 * d * ADA_TN * 4),
        name="adaln",
    )(c, w_ada, b_ada.reshape(1, n))


def _cast_blocking(rows, cols, n_i, n_j):
    lane_groups = cols // 128
    n_c = max(k for k in range(1, n_j + 1) if lane_groups % k == 0)
    assert rows % (n_i * BF16_SUBLANE_TILE) == 0 and cols % 128 == 0
    return rows // n_i, cols // n_c, n_c


def _side_cast_specs(cast, layer, n_i, n_j):
    blockings = [_cast_blocking(w.shape[1], w.shape[2], n_i, n_j) for w in cast]

    def cast_map(n_c, lead):
        return lambda i, j: lead + (i, jnp.minimum(j, n_c - 1))

    in_specs = [pl.BlockSpec((None, r, c), cast_map(n_c, (layer,))) for r, c, n_c in blockings]
    out_specs = [pl.BlockSpec((r, c), cast_map(n_c, ())) for r, c, n_c in blockings]
    out_shapes = [jax.ShapeDtypeStruct(w.shape[1:], BF16) for w in cast]
    return in_specs, out_specs, out_shapes, sum(2 * r * c * 6 for r, c, _ in blockings)


def _split_cast_refs(rest, n_cast):
    return rest[:n_cast], rest[n_cast], rest[n_cast + 1:2 * n_cast + 1], rest[2 * n_cast + 1:]


def _side_cast(cast_in, cast_out):
    for src_ref, dst_ref in zip(cast_in, cast_out):
        dst_ref[...] = src_ref[...].astype(BF16)


def _ffn_step(first, last, x_ref, mod_ref, h_ref, o_ref, weights, shift_row):
    if first:
        for r in range(x_ref.shape[0] // ROW_CHUNK):
            _rms_mod_rows(x_ref, mod_ref, h_ref, shift_row, pl.ds(r * ROW_CHUNK, ROW_CHUNK))
    wg, wu, wd = weights()
    h = h_ref[...]
    g = jnp.dot(h, wg, preferred_element_type=F32)
    u = jnp.dot(h, wu, preferred_element_type=F32)
    act = (g * jax.nn.sigmoid(g) * u).astype(BF16)
    part = jnp.dot(act, wd, preferred_element_type=F32)
    if first:
        o_ref[...] = part
    elif last:
        gate = FFN_RES * mod_ref[0, shift_row + 2:shift_row + 3, :]
        o_ref[...] = x_ref[...] + gate * (o_ref[...] + part)
    else:
        o_ref[...] += part


def _ffn_variants(active, f, n_f, step):
    def when(cond):
        return pl.when(cond if active is None else jnp.logical_and(active, cond))
    when(f == 0)(functools.partial(step, True, False))
    when(jnp.logical_and(f > 0, f < n_f - 1))(functools.partial(step, False, False))
    when(f == n_f - 1)(functools.partial(step, False, True))


def _ffn_head_kernel(x_ref, mod_ref, wg32_ref, wu32_ref, wd32_ref, o_ref, wg16_ref, wu16_ref, wd16_ref,
                     h_ref, *, shift_row, n_f):
    def weights():
        blocks = []
        for src_ref, dst_ref in ((wg32_ref, wg16_ref), (wu32_ref, wu16_ref), (wd32_ref, wd16_ref)):
            w = src_ref[...].astype(BF16)
            dst_ref[...] = w
            blocks.append(w)
        return blocks

    def step(first, last):
        _ffn_step(first, last, x_ref, mod_ref, h_ref, o_ref, weights, shift_row)

    _ffn_variants(None, pl.program_id(1), n_f, step)


def _ffn_head(x2d, mod, w_gu, w_down, *, shift_row, layer):
    d = x2d.shape[1]
    d_ff = w_down.shape[1]
    tm, tf = FFN_TM, FFN_HEAD_TF
    n_f = d_ff // tf
    assert n_f >= 3
    once = pl.Buffered(1)
    vmem = 2 * tm * d * 4 + tm * d * 2 + 2 * 3 * d * tf * 6 + 6 * tm * tf * 4 + (4 << 20)
    return pl.pallas_call(
        functools.partial(_ffn_head_kernel, shift_row=shift_row, n_f=n_f),
        grid=(1, n_f),
        in_specs=[pl.BlockSpec((tm, d), lambda i, f: (0, 0), pipeline_mode=once),
                  pl.BlockSpec((1, N_MOD, d), lambda i, f: (0, 0, 0)),
                  pl.BlockSpec((None, d, tf), lambda i, f: (layer, 0, f)),
                  pl.BlockSpec((None, d, tf), lambda i, f: (layer, 0, f + n_f)),
                  pl.BlockSpec((None, tf, d), lambda i, f: (layer, f, 0))],
        out_specs=[pl.BlockSpec((tm, d), lambda i, f: (0, 0)),
                   pl.BlockSpec((d, tf), lambda i, f: (0, f)),
                   pl.BlockSpec((d, tf), lambda i, f: (0, f)),
                   pl.BlockSpec((tf, d), lambda i, f: (f, 0))],
        out_shape=[jax.ShapeDtypeStruct((tm, d), F32),
                   jax.ShapeDtypeStruct((d, d_ff), BF16),
                   jax.ShapeDtypeStruct((d, d_ff), BF16),
                   jax.ShapeDtypeStruct((d_ff, d), BF16)],
        scratch_shapes=[pltpu.VMEM((tm, d), BF16)],
        compiler_params=_params(("arbitrary", "arbitrary"), vmem),
        name="ffn_head",
    )(x2d, mod, w_gu, w_gu, w_down)


def _ffn_kernel(x_ref, mod_ref, wg_ref, wu_ref, wd_ref, *rest, shift_row, n_f, n_cast, has_head):
    head_ref, rest = (rest[0], rest[1:]) if has_head else (None, rest)
    cast_in, o_ref, cast_out, (h_ref,) = _split_cast_refs(rest, n_cast)
    i = pl.program_id(0)
    f = pl.program_id(1)

    def step(first, last):
        _ffn_step(first, last, x_ref, mod_ref, h_ref, o_ref,
                  lambda: (wg_ref[...], wu_ref[...], wd_ref[...]), shift_row)
        _side_cast(cast_in, cast_out)

    _ffn_variants(i > 0 if has_head else None, f, n_f, step)

    if has_head:
        @pl.when(i == 0)
        def _():
            _side_cast(cast_in, cast_out)

        @pl.when(jnp.logical_and(i == 0, f == n_f - 1))
        def _():
            pltpu.sync_copy(head_ref, o_ref)


def _ffn(x2d, mod, w_g, w_u, w_down, *, shift_row, seq, head=None, cast=(), layer=0):
    t, d = x2d.shape
    d_ff = w_down.shape[0]
    n_f = d_ff // FFN_TF
    assert n_f >= 3
    tm = FFN_TM
    n_i = t // tm
    has_head = head is not None
    (wg_arr, g_off), (wu_arr, u_off) = w_g, w_u

    def col(f, i):
        return jnp.where(i == 0, 0, f) if has_head else f

    cast_in_specs, cast_out_specs, cast_shapes, cast_vmem = _side_cast_specs(cast, layer, n_i, n_f)
    vmem = ((2 * tm * d * 4) * 2 + tm * d * 2 + 2 * 3 * d * FFN_TF * 2 + 6 * tm * FFN_TF * 4
            + cast_vmem + (4 << 20))
    outs = pl.pallas_call(
        functools.partial(_ffn_kernel, shift_row=shift_row, n_f=n_f, n_cast=len(cast), has_head=has_head),
        grid=(n_i, n_f),
        in_specs=[pl.BlockSpec((tm, d), lambda i, f: (jnp.maximum(i, 1) if has_head else i, 0)),
                  pl.BlockSpec((1, N_MOD, d), lambda i, f: (i * tm // seq, 0, 0)),
                  pl.BlockSpec((d, FFN_TF), lambda i, f: (0, g_off + col(f, i))),
                  pl.BlockSpec((d, FFN_TF), lambda i, f: (0, u_off + col(f, i))),
                  pl.BlockSpec((FFN_TF, d), lambda i, f: (col(f, i), 0))]
                 + ([pl.BlockSpec(memory_space=pl.ANY)] if has_head else []) + cast_in_specs,
        out_specs=[pl.BlockSpec((tm, d), lambda i, f: (i, 0))] + cast_out_specs,
        out_shape=[jax.ShapeDtypeStruct((t, d), F32)] + cast_shapes,
        scratch_shapes=[pltpu.VMEM((tm, d), BF16)],
        compiler_params=_params(("arbitrary", "arbitrary"), vmem),
        name="ffn",
    )(x2d, mod, wg_arr, wu_arr, w_down, *((head,) if has_head else ()), *cast)
    return outs[0], outs[1:]


def _head_rms_gain(p, gain):
    outs = []
    for hd in range(p.shape[1] // HEAD_DIM):
        ph = p[:, hd * HEAD_DIM:(hd + 1) * HEAD_DIM]
        ms = jnp.mean(ph * ph, axis=-1, keepdims=True)
        outs.append(ph * lax.rsqrt(ms + EPS) * gain)
    return jnp.concatenate(outs, axis=1)


def _in_proj_kernel(x_ref, mod_ref, w_ref, qg_ref, kg_ref, *rest, shift_row, q_scale, n_cast):
    cast_in, o_ref, cast_out, (h_ref,) = _split_cast_refs(rest, n_cast)
    n = pl.program_id(1)
    tm = x_ref.shape[0]

    @pl.when(n == 0)
    def _():
        for r in range(tm // ROW_CHUNK):
            _rms_mod_rows(x_ref, mod_ref, h_ref, shift_row, pl.ds(r * ROW_CHUNK, ROW_CHUNK))
        p = jnp.dot(h_ref[...], w_ref[...], preferred_element_type=F32)
        o_ref[...] = _head_rms_gain(p, qg_ref[...] * q_scale).astype(BF16)
        _side_cast(cast_in, cast_out)

    @pl.when(n == 1)
    def _():
        p = jnp.dot(h_ref[...], w_ref[...], preferred_element_type=F32)
        o_ref[...] = _head_rms_gain(p, kg_ref[...]).astype(BF16)
        _side_cast(cast_in, cast_out)

    @pl.when(n >= 2)
    def _():
        o_ref[...] = jnp.dot(h_ref[...], w_ref[...], preferred_element_type=F32).astype(BF16)
        _side_cast(cast_in, cast_out)


def _in_proj(x2d, mod, w_in, q_gain, k_gain, *, shift_row, seq, cast=(), layer=0):
    t, d = x2d.shape
    n_cols = w_in.shape[1]
    tm = PROJ_TM
    n_i, n_n = t // tm, n_cols // PROJ_TN
    cast_in_specs, cast_out_specs, cast_shapes, cast_vmem = _side_cast_specs(cast, layer, n_i, n_n)
    vmem = (2 * tm * d * 4 + tm * d * 2 + 2 * d * PROJ_TN * 2 + 2 * tm * PROJ_TN * 2 + 6 * tm * PROJ_TN * 4
            + cast_vmem + (4 << 20))
    q_scale = LOG2E / math.sqrt(HEAD_DIM)
    outs = pl.pallas_call(
        functools.partial(_in_proj_kernel, shift_row=shift_row, q_scale=q_scale, n_cast=len(cast)),
        grid=(n_i, n_n),
        in_specs=[pl.BlockSpec((tm, d), lambda i, n: (i, 0)),
                  pl.BlockSpec((1, N_MOD, d), lambda i, n: (i * tm // seq, 0, 0)),
                  pl.BlockSpec((d, PROJ_TN), lambda i, n: (0, n)),
                  pl.BlockSpec((1, HEAD_DIM), lambda i, n: (0, 0)),
                  pl.BlockSpec((1, HEAD_DIM), lambda i, n: (0, 0))] + cast_in_specs,
        out_specs=[pl.BlockSpec((tm, PROJ_TN), lambda i, n: (i, n))] + cast_out_specs,
        out_shape=[jax.ShapeDtypeStruct((t, n_cols), BF16)] + cast_shapes,
        scratch_shapes=[pltpu.VMEM((tm, d), BF16)],
        compiler_params=_params(("arbitrary", "arbitrary"), vmem),
        name="in_proj",
    )(x2d, mod, w_in, q_gain, k_gain, *cast)
    return outs[0], outs[1:]


def _attn_step(chains, neg_tri, fillers=()):
    n_c = len(chains)
    z, sp16, rs, neg_incl, a, pv = ({} for _ in range(6))

    def scores(c):
        q, k = chains[c][0], chains[c][1]
        z[c] = lax.dot_general(q, k, (((1,), (1,)), ((), ())), preferred_element_type=F32)

    def softplus(c):
        mask = chains[c][3]
        sp = jnp.maximum(z[c], 0.0) + jnp.log(1.0 + jnp.exp2(jnp.minimum(z[c], -z[c]))) * LOG2E
        if mask is not None:
            sp = jnp.where(mask, sp, 0.0)
        sp16[c] = sp.astype(BF16)
        rs[c] = jnp.sum(sp, axis=-1, keepdims=True)

    def suffix_sum(c):
        neg_incl[c] = jnp.dot(sp16[c], neg_tri, preferred_element_type=F32)

    def weights(c):
        mask, carry_fn = chains[c][3], chains[c][4]
        w = jnp.exp2(z[c] + neg_incl[c] + carry_fn(rs))
        if mask is not None:
            w = jnp.where(mask, w, 0.0)
        a[c] = w.astype(BF16)

    def values(c):
        pv[c] = jnp.dot(a[c], chains[c][2], preferred_element_type=F32)

    stages = (scores, softplus, suffix_sum, weights, values)
    n_waves = n_c + len(stages) - 1
    done_fill = 0
    for wave in range(n_waves):
        for s in range(len(stages)):
            c = wave - s
            if 0 <= c < n_c:
                stages[s](c)
        while done_fill < (wave + 1) * len(fillers) // n_waves:
            fillers[done_fill]()
            done_fill += 1
    return [pv[c] for c in range(n_c)], [rs[c] for c in range(n_c)]


def _mixer_kernel(qkv_ref, kp_ref, vp_ref, kv_hbm, x_ref, mod_ref, bcu_ref, cuh_ref, cw_ref, w_ref, *rest,
                  gate_row, nq, n_cast):
    cast_in, o_ref, cast_out, (attn_ref, cu_ref, kbuf_ref, vbuf_ref, kv_sem) = _split_cast_refs(rest, n_cast)
    b = pl.program_id(0)
    i = pl.program_id(1)
    t = qkv_ref.shape[0]
    halo = cuh_ref.shape[0]
    width = attn_ref.shape[1]
    n_h = width // HEAD_DIM
    d = o_ref.shape[1]
    q_ref, kd_ref, vd_ref = (qkv_ref.at[:, n * width:(n + 1) * width] for n in range(3))
    gb_ref, gc_ref, u_ref = (bcu_ref.at[:, n * width:(n + 1) * width] for n in range(3))
    gch_ref, uh_ref = (cuh_ref.at[:, n * width:(n + 1) * width] for n in range(2))

    def out_proj_chunks():
        attn_prev = attn_ref[...]
        prev = gch_ref[...].astype(F32) * uh_ref[...].astype(F32)
        cu_ref[0:halo, :] = jnp.where(i == 1, 0.0, prev)
        cu_ref[halo:halo + t, :] = gc_ref[...].astype(F32) * u_ref[...].astype(F32)
        conv = cu_ref[halo:halo + t, :] * cw_ref[CONV_K - 1:CONV_K, :]
        for k in range(CONV_K - 1):
            back = CONV_K - 1 - k
            conv = conv + cu_ref[halo - back:halo - back + t, :] * cw_ref[k:k + 1, :]
        y = (gb_ref[...].astype(F32) * conv).astype(BF16)
        gate = mod_ref[0, gate_row:gate_row + 1, :]

        def out_chunk(c):
            cols = slice(c * OUT_TN, (c + 1) * OUT_TN)
            mix = jnp.dot(attn_prev, w_ref[0:width, cols], preferred_element_type=F32)
            mix = mix + jnp.dot(y, w_ref[width:, cols], preferred_element_type=F32)
            o_ref[:, cols] = x_ref[:, cols] + gate[:, cols] * mix

        return [functools.partial(out_chunk, c) for c in range(d // OUT_TN)]

    def attention(fillers, has_prev):
        row = lax.broadcasted_iota(jnp.int32, (t, t), 0)
        col = lax.broadcasted_iota(jnp.int32, (t, t), 1)
        neg_tri = jnp.where(row >= col, -1.0, 0.0).astype(BF16)
        causal = col < row

        def lanes(hd):
            return slice(hd * HEAD_DIM, (hd + 1) * HEAD_DIM)

        def heads_of(kblk_ref, vblk_ref):
            return [(kblk_ref[:, lanes(hd)], vblk_ref[:, lanes(hd)]) for hd in range(n_h)]

        def fetch_key_block(j):
            rows = pl.ds(pl.multiple_of((b * nq + j) * t, t), t)
            copies = [pltpu.make_async_copy(kv_hbm.at[rows, pl.ds(col * width, width)], buf, kv_sem.at[n])
                      for n, (col, buf) in enumerate(((K_COL, kbuf_ref), (V_COL, vbuf_ref)))]
            for cp in copies:
                cp.start()
            for cp in copies:
                cp.wait()

        qs = [q_ref[:, lanes(hd)] for hd in range(n_h)]
        zero = jnp.zeros((t, 1), F32)
        kv = heads_of(kd_ref, vd_ref)
        chains = [(qs[hd], kv[hd][0], kv[hd][1], causal, lambda rs: zero) for hd in range(n_h)]
        if has_prev:
            kv = heads_of(kp_ref, vp_ref)
            chains += [(qs[hd], kv[hd][0], kv[hd][1], None, lambda rs, hd=hd: -rs[hd]) for hd in range(n_h)]
        pvs, rss = _attn_step(chains, neg_tri, fillers)
        if has_prev:
            accs = tuple(pvs[hd] + pvs[n_h + hd] for hd in range(n_h))
            carries = tuple(-rss[hd] - rss[n_h + hd] for hd in range(n_h))

            def live_max(carries):
                live = carries[0]
                for cr in carries[1:]:
                    live = jnp.maximum(live, cr)
                return jnp.max(live)

            def cond(state):
                jj, live = state[0], state[1]
                return jnp.logical_and(jj < i, live > F32_EXP2_UNDERFLOW)

            def body(state):
                jj, _, accs, carries = state
                fetch_key_block(i - 1 - jj)
                kv = heads_of(kbuf_ref, vbuf_ref)
                pvs, rss = _attn_step([(qs[hd], kv[hd][0], kv[hd][1], None, lambda rs, hd=hd: carries[hd])
                                       for hd in range(n_h)], neg_tri)
                accs = tuple(acc + pv for acc, pv in zip(accs, pvs))
                carries = tuple(cr - rs for cr, rs in zip(carries, rss))
                return jj + 1, live_max(carries), accs, carries

            _, _, accs, _ = lax.while_loop(cond, body, (jnp.int32(1), live_max(carries), accs, carries))
        else:
            accs = pvs
        for hd in range(n_h):
            attn_ref[:, lanes(hd)] = accs[hd].astype(attn_ref.dtype)

    @pl.when(i == 0)
    def _():
        attention((), has_prev=False)
        _side_cast(cast_in, cast_out)

    @pl.when(jnp.logical_and(i > 0, i < nq))
    def _():
        attention(out_proj_chunks(), has_prev=True)
        _side_cast(cast_in, cast_out)

    @pl.when(i == nq)
    def _():
        for job in out_proj_chunks():
            job()
        _side_cast(cast_in, cast_out)


def _mixer(x2d, mod, proj, conv_w, w_out, *, gate_row, bsz, seq, n_heads, cast=(), layer=0):
    t_all, d = x2d.shape
    width = n_heads * HEAD_DIM
    t = ATT_T
    nq = seq // t
    halo = BF16_SUBLANE_TILE
    per = t // halo

    def q_row(b, i):
        return b * nq + jnp.minimum(i, nq - 1)

    def prev_row(b, i):
        return b * nq + jnp.maximum(jnp.minimum(i, nq - 1) - 1, 0)

    def out_row(b, i):
        return b * nq + jnp.maximum(i - 1, 0)

    cast_in_specs, cast_out_specs, cast_shapes, cast_vmem = _side_cast_specs(cast, layer, bsz, nq + 1)
    vmem = (2 * 5 * t * width * 2 + 2 * t * width * 2 + d * d * 2 + 2 * 2 * t * d * 4 + 2 * 4 * t * width * 2
            + (t + halo) * width * 4 + 2 * n_heads * 12 * t * t * 4 + cast_vmem + (4 << 20))
    outs = pl.pallas_call(
        functools.partial(_mixer_kernel, gate_row=gate_row, nq=nq, n_cast=len(cast)),
        grid=(bsz, nq + 1),
        in_specs=[pl.BlockSpec((t, 3 * width), lambda b, i: (q_row(b, i), 0)),
                  pl.BlockSpec((t, width), lambda b, i: (prev_row(b, i), K_COL)),
                  pl.BlockSpec((t, width), lambda b, i: (prev_row(b, i), V_COL)),
                  pl.BlockSpec(memory_space=pl.ANY),
                  pl.BlockSpec((t, d), lambda b, i: (out_row(b, i), 0)),
                  pl.BlockSpec((1, N_MOD, d), lambda b, i: (b, 0, 0)),
                  pl.BlockSpec((t, 3 * width), lambda b, i: (out_row(b, i), 1)),
                  pl.BlockSpec((halo, 2 * width), lambda b, i: (jnp.maximum(out_row(b, i) * per - 1, 0), 2)),
                  pl.BlockSpec((CONV_K, width), lambda b, i: (0, 0)),
                  pl.BlockSpec((d, d), lambda b, i: (0, 0))] + cast_in_specs,
        out_specs=[pl.BlockSpec((t, d), lambda b, i: (out_row(b, i), 0))] + cast_out_specs,
        out_shape=[jax.ShapeDtypeStruct((t_all, d), F32)] + cast_shapes,
        scratch_shapes=[pltpu.VMEM((t, width), BF16), pltpu.VMEM((t + halo, width), F32),
                        pltpu.VMEM((t, width), BF16), pltpu.VMEM((t, width), BF16),
                        pltpu.SemaphoreType.DMA((2,))],
        compiler_params=_params(("arbitrary", "arbitrary"), vmem),
        name="mixer",
    )(proj, proj, proj, proj, x2d, mod, proj, proj, conv_w, w_out, *cast)
    return outs[0], outs[1:]


def kernel(x, c, w_ada, b_ada, w1_gu, w1_down, w_in, q_norm_w, k_norm_w, conv_w, w_out, w2_gu, w2_down):
    bsz, seq, d = x.shape
    depth = w_ada.shape[0]
    width = conv_w.shape[2]
    n_heads = width // HEAD_DIM
    assert w_in.shape[2] == 6 * width and width == PROJ_TN
    assert seq % max(FFN_TM, PROJ_TM, ATT_T) == 0 and d % OUT_TN == 0

    x2d = x.reshape(bsz * seq, d)
    for l in range(depth):
        mod = _adaln(c, w_ada[l], b_ada[l]).reshape(bsz, N_MOD, d)
        head, w1_g_b, w1_u_b, w1_down_b = _ffn_head(x2d, mod, w1_gu, w1_down, shift_row=0, layer=l)
        x2d, (w_in_b, w_out_b) = _ffn(
            x2d, mod, (w1_g_b, 0), (w1_u_b, 0), w1_down_b, shift_row=0, seq=seq, head=head,
            cast=(w_in, w_out), layer=l)
        proj, (w2_gu_b,) = _in_proj(
            x2d, mod, w_in_b, q_norm_w[l].reshape(1, HEAD_DIM), k_norm_w[l].reshape(1, HEAD_DIM),
            shift_row=3, seq=seq, cast=(w2_gu,), layer=l)
        x2d, (w2_down_b,) = _mixer(x2d, mod, proj, conv_w[l], w_out_b, gate_row=5, bsz=bsz, seq=seq,
                                   n_heads=n_heads, cast=(w2_down,), layer=l)
        n_f = w2_down_b.shape[0] // FFN_TF
        x2d, _ = _ffn(x2d, mod, (w2_gu_b, 0), (w2_gu_b, n_f), w2_down_b, shift_row=6, seq=seq)
    return x2d.reshape(bsz, seq, d)
```

```python
import functools
import math

import jax
import jax.numpy as jnp
from jax import lax
from jax.experimental import pallas as pl
from jax.experimental.pallas import tpu as pltpu

F32 = jnp.float32
BF16 = jnp.bfloat16

HEAD_DIM = 128
K_COL, V_COL = 1, 2
CONV_K = 3
N_MOD = 9
FFN_RES = 0.5
EPS = 1e-6
LOG2E = 1.4426950408889634

F32_EXP2_UNDERFLOW = -150.0

V7X_VMEM_LIMIT_BYTES = 58 * 1024 * 1024
BF16_SUBLANE_TILE = 16

ADA_TN = 1024
FFN_TM = 1024
FFN_TF = 512
FFN_HEAD_TF = 256
PROJ_TM = 1024
PROJ_TN = 1024
ATT_T = 256
OUT_TN = 256
ROW_CHUNK = 256


def _params(semantics, vmem_bytes):
    return pltpu.CompilerParams(dimension_semantics=semantics,
                                vmem_limit_bytes=min(int(vmem_bytes), V7X_VMEM_LIMIT_BYTES))


def _rms_mod_rows(x_ref, mod_ref, h_ref, shift_row, rows):
    xv = x_ref[rows, :]
    ms = jnp.mean(xv * xv, axis=-1, keepdims=True)
    hn = xv * lax.rsqrt(ms + EPS)
    shift = mod_ref[0, shift_row:shift_row + 1, :]
    scale = mod_ref[0, shift_row + 1:shift_row + 2, :]
    h_ref[rows, :] = (hn * (1.0 + scale) + shift).astype(BF16)


def _adaln_kernel(c_ref, w_ref, b_ref, o_ref):
    cv = c_ref[...]
    c_act = (cv * jax.nn.sigmoid(cv)).astype(BF16)
    o_ref[...] = jnp.dot(c_act, w_ref[...].astype(BF16), preferred_element_type=F32) + b_ref[...]


def _adaln(c, w_ada, b_ada):
    bsz, d = c.shape
    n = w_ada.shape[1]
    return pl.pallas_call(
        _adaln_kernel,
        grid=(n // ADA_TN,),
        in_specs=[pl.BlockSpec((bsz, d), lambda j: (0, 0)),
                  pl.BlockSpec((d, ADA_TN), lambda j: (0, j)),
                  pl.BlockSpec((1, ADA_TN), lambda j: (0, j))],
        out_specs=pl.BlockSpec((bsz, ADA_TN), lambda j: (0, j)),
        out_shape=jax.ShapeDtypeStruct((bsz, n), F32),
        compiler_params=_params(("arbitrary",), 4 * d * ADA_TN * 4),
        name="adaln",
    )(c, w_ada, b_ada.reshape(1, n))


def _cast_blocking(rows, cols, n_i, n_j):
    lane_groups = cols // 128
    n_c = max(k for k in range(1, n_j + 1) if lane_groups % k == 0)
    assert rows % (n_i * BF16_SUBLANE_TILE) == 0 and cols % 128 == 0
    return rows // n_i, cols // n_c, n_c


def _side_cast_specs(cast, layer, n_i, n_j):
    blockings = [_cast_blocking(w.shape[1], w.shape[2], n_i, n_j) for w in cast]

    def cast_map(n_c, lead):
        return lambda i, j: lead + (i, jnp.minimum(j, n_c - 1))

    in_specs = [pl.BlockSpec((None, r, c), cast_map(n_c, (layer,))) for r, c, n_c in blockings]
    out_specs = [pl.BlockSpec((r, c), cast_map(n_c, ())) for r, c, n_c in blockings]
    out_shapes = [jax.ShapeDtypeStruct(w.shape[1:], BF16) for w in cast]
    return in_specs, out_specs, out_shapes, sum(2 * r * c * 6 for r, c, _ in blockings)


def _split_cast_refs(rest, n_cast):
    return rest[:n_cast], rest[n_cast], rest[n_cast + 1:2 * n_cast + 1], rest[2 * n_cast + 1:]


def _side_cast(cast_in, cast_out):
    for src_ref, dst_ref in zip(cast_in, cast_out):
        dst_ref[...] = src_ref[...].astype(BF16)


def _ffn_step(first, last, x_ref, mod_ref, h_ref, o_ref, weights, shift_row):
    if first:
        for r in range(x_ref.shape[0] // ROW_CHUNK):
            _rms_mod_rows(x_ref, mod_ref, h_ref, shift_row, pl.ds(r * ROW_CHUNK, ROW_CHUNK))
    wg, wu, wd = weights()
    h = h_ref[...]
    g = jnp.dot(h, wg, preferred_element_type=F32)
    u = jnp.dot(h, wu, preferred_element_type=F32)
    act = (g * jax.nn.sigmoid(g) * u).astype(BF16)
    part = jnp.dot(act, wd, preferred_element_type=F32)
    if first:
        o_ref[...] = part
    elif last:
        gate = FFN_RES * mod_ref[0, shift_row + 2:shift_row + 3, :]
        o_ref[...] = x_ref[...] + gate * (o_ref[...] + part)
    else:
        o_ref[...] += part


def _ffn_variants(active, f, n_f, step):
    def when(cond):
        return pl.when(cond if active is None else jnp.logical_and(active, cond))
    when(f == 0)(functools.partial(step, True, False))
    when(jnp.logical_and(f > 0, f < n_f - 1))(functools.partial(step, False, False))
    when(f == n_f - 1)(functools.partial(step, False, True))


def _ffn_head_kernel(x_ref, mod_ref, wg32_ref, wu32_ref, wd32_ref, o_ref, wg16_ref, wu16_ref, wd16_ref,
                     h_ref, *, shift_row, n_f):
    def weights():
        blocks = []
        for src_ref, dst_ref in ((wg32_ref, wg16_ref), (wu32_ref, wu16_ref), (wd32_ref, wd16_ref)):
            w = src_ref[...].astype(BF16)
            dst_ref[...] = w
            blocks.append(w)
        return blocks

    def step(first, last):
        _ffn_step(first, last, x_ref, mod_ref, h_ref, o_ref, weights, shift_row)

    _ffn_variants(None, pl.program_id(1), n_f, step)


def _ffn_head(x2d, mod, w_gu, w_down, *, shift_row, layer):
    d = x2d.shape[1]
    d_ff = w_down.shape[1]
    tm, tf = FFN_TM, FFN_HEAD_TF
    n_f = d_ff // tf
    assert n_f >= 3
    once = pl.Buffered(1)
    vmem = 2 * tm * d * 4 + tm * d * 2 + 2 * 3 * d * tf * 6 + 6 * tm * tf * 4 + (4 << 20)
    return pl.pallas_call(
        functools.partial(_ffn_head_kernel, shift_row=shift_row, n_f=n_f),
        grid=(1, n_f),
        in_specs=[pl.BlockSpec((tm, d), lambda i, f: (0, 0), pipeline_mode=once),
                  pl.BlockSpec((1, N_MOD, d), lambda i, f: (0, 0, 0)),
                  pl.BlockSpec((None, d, tf), lambda i, f: (layer, 0, f)),
                  pl.BlockSpec((None, d, tf), lambda i, f: (layer, 0, f + n_f)),
                  pl.BlockSpec((None, tf, d), lambda i, f: (layer, f, 0))],
        out_specs=[pl.BlockSpec((tm, d), lambda i, f: (0, 0)),
                   pl.BlockSpec((d, tf), lambda i, f: (0, f)),
                   pl.BlockSpec((d, tf), lambda i, f: (0, f)),
                   pl.BlockSpec((tf, d), lambda i, f: (f, 0))],
        out_shape=[jax.ShapeDtypeStruct((tm, d), F32),
                   jax.ShapeDtypeStruct((d, d_ff), BF16),
                   jax.ShapeDtypeStruct((d, d_ff), BF16),
                   jax.ShapeDtypeStruct((d_ff, d), BF16)],
        scratch_shapes=[pltpu.VMEM((tm, d), BF16)],
        compiler_params=_params(("arbitrary", "arbitrary"), vmem),
        name="ffn_head",
    )(x2d, mod, w_gu, w_gu, w_down)


def _ffn_kernel(x_ref, mod_ref, wg_ref, wu_ref, wd_ref, *rest, shift_row, n_f, n_cast, has_head):
    head_ref, rest = (rest[0], rest[1:]) if has_head else (None, rest)
    cast_in, o_ref, cast_out, (h_ref,) = _split_cast_refs(rest, n_cast)
    i = pl.program_id(0)
    f = pl.program_id(1)

    def step(first, last):
        _ffn_step(first, last, x_ref, mod_ref, h_ref, o_ref,
                  lambda: (wg_ref[...], wu_ref[...], wd_ref[...]), shift_row)
        _side_cast(cast_in, cast_out)

    _ffn_variants(i > 0 if has_head else None, f, n_f, step)

    if has_head:
        @pl.when(i == 0)
        def _():
            _side_cast(cast_in, cast_out)

        @pl.when(jnp.logical_and(i == 0, f == n_f - 1))
        def _():
            pltpu.sync_copy(head_ref, o_ref)


def _ffn(x2d, mod, w_g, w_u, w_down, *, shift_row, seq, head=None, cast=(), layer=0):
    t, d = x2d.shape
    d_ff = w_down.shape[0]
    n_f = d_ff // FFN_TF
    assert n_f >= 3
    tm = FFN_TM
    n_i = t // tm
    has_head = head is not None
    (wg_arr, g_off), (wu_arr, u_off) = w_g, w_u

    def col(f, i):
        return jnp.where(i == 0, 0, f) if has_head else f

    cast_in_specs, cast_out_specs, cast_shapes, cast_vmem = _side_cast_specs(cast, layer, n_i, n_f)
    vmem = ((2 * tm * d * 4) * 2 + tm * d * 2 + 2 * 3 * d * FFN_TF * 2 + 6 * tm * FFN_TF * 4
            + cast_vmem + (4 << 20))
    outs = pl.pallas_call(
        functools.partial(_ffn_kernel, shift_row=shift_row, n_f=n_f, n_cast=len(cast), has_head=has_head),
        grid=(n_i, n_f),
        in_specs=[pl.BlockSpec((tm, d), lambda i, f: (jnp.maximum(i, 1) if has_head else i, 0)),
                  pl.BlockSpec((1, N_MOD, d), lambda i, f: (i * tm // seq, 0, 0)),
                  pl.BlockSpec((d, FFN_TF), lambda i, f: (0, g_off + col(f, i))),
                  pl.BlockSpec((d, FFN_TF), lambda i, f: (0, u_off + col(f, i))),
                  pl.BlockSpec((FFN_TF, d), lambda i, f: (col(f, i), 0))]
                 + ([pl.BlockSpec(memory_space=pl.ANY)] if has_head else []) + cast_in_specs,
        out_specs=[pl.BlockSpec((tm, d), lambda i, f: (i, 0))] + cast_out_specs,
        out_shape=[jax.ShapeDtypeStruct((t, d), F32)] + cast_shapes,
        scratch_shapes=[pltpu.VMEM((tm, d), BF16)],
        compiler_params=_params(("arbitrary", "arbitrary"), vmem),
        name="ffn",
    )(x2d, mod, wg_arr, wu_arr, w_down, *((head,) if has_head else ()), *cast)
    return outs[0], outs[1:]


def _head_rms_gain(p, gain):
    outs = []
    for hd in range(p.shape[1] // HEAD_DIM):
        ph = p[:, hd * HEAD_DIM:(hd + 1) * HEAD_DIM]
        ms = jnp.mean(ph * ph, axis=-1, keepdims=True)
        outs.append(ph * lax.rsqrt(ms + EPS) * gain)
    return jnp.concatenate(outs, axis=1)


def _in_proj_kernel(x_ref, mod_ref, w_ref, qg_ref, kg_ref, *rest, shift_row, q_scale, n_cast):
    cast_in, o_ref, cast_out, (h_ref,) = _split_cast_refs(rest, n_cast)
    n = pl.program_id(1)
    tm = x_ref.shape[0]

    @pl.when(n == 0)
    def _():
        for r in range(tm // ROW_CHUNK):
            _rms_mod_rows(x_ref, mod_ref, h_ref, shift_row, pl.ds(r * ROW_CHUNK, ROW_CHUNK))
        p = jnp.dot(h_ref[...], w_ref[...], preferred_element_type=F32)
        o_ref[...] = _head_rms_gain(p, qg_ref[...] * q_scale).astype(BF16)
        _side_cast(cast_in, cast_out)

    @pl.when(n == 1)
    def _():
        p = jnp.dot(h_ref[...], w_ref[...], preferred_element_type=F32)
        o_ref[...] = _head_rms_gain(p, kg_ref[...]).astype(BF16)
        _side_cast(cast_in, cast_out)

    @pl.when(n >= 2)
    def _():
        o_ref[...] = jnp.dot(h_ref[...], w_ref[...], preferred_element_type=F32).astype(BF16)
        _side_cast(cast_in, cast_out)


def _in_proj(x2d, mod, w_in, q_gain, k_gain, *, shift_row, seq, cast=(), layer=0):
    t, d = x2d.shape
    n_cols = w_in.shape[1]
    tm = PROJ_TM
    n_i, n_n = t // tm, n_cols // PROJ_TN
    cast_in_specs, cast_out_specs, cast_shapes, cast_vmem = _side_cast_specs(cast, layer, n_i, n_n)
    vmem = (2 * tm * d * 4 + tm * d * 2 + 2 * d * PROJ_TN * 2 + 2 * tm * PROJ_TN * 2 + 6 * tm * PROJ_TN * 4
            + cast_vmem + (4 << 20))
    q_scale = LOG2E / math.sqrt(HEAD_DIM)
    outs = pl.pallas_call(
        functools.partial(_in_proj_kernel, shift_row=shift_row, q_scale=q_scale, n_cast=len(cast)),
        grid=(n_i, n_n),
        in_specs=[pl.BlockSpec((tm, d), lambda i, n: (i, 0)),
                  pl.BlockSpec((1, N_MOD, d), lambda i, n: (i * tm // seq, 0, 0)),
                  pl.BlockSpec((d, PROJ_TN), lambda i, n: (0, n)),
                  pl.BlockSpec((1, HEAD_DIM), lambda i, n: (0, 0)),
                  pl.BlockSpec((1, HEAD_DIM), lambda i, n: (0, 0))] + cast_in_specs,
        out_specs=[pl.BlockSpec((tm, PROJ_TN), lambda i, n: (i, n))] + cast_out_specs,
        out_shape=[jax.ShapeDtypeStruct((t, n_cols), BF16)] + cast_shapes,
        scratch_shapes=[pltpu.VMEM((tm, d), BF16)],
        compiler_params=_params(("arbitrary", "arbitrary"), vmem),
        name="in_proj",
    )(x2d, mod, w_in, q_gain, k_gain, *cast)
    return outs[0], outs[1:]


def _attn_step(chains, neg_tri, fillers=()):
    n_c = len(chains)
    z, sp16, rs, neg_incl, a, pv = ({} for _ in range(6))

    def scores(c):
        q, k = chains[c][0], chains[c][1]
        z[c] = lax.dot_general(q, k, (((1,), (1,)), ((), ())), preferred_element_type=F32)

    def softplus(c):
        mask = chains[c][3]
        sp = jnp.maximum(z[c], 0.0) + jnp.log(1.0 + jnp.exp2(jnp.minimum(z[c], -z[c]))) * LOG2E
        if mask is not None:
            sp = jnp.where(mask, sp, 0.0)
        sp16[c] = sp.astype(BF16)
        rs[c] = jnp.sum(sp, axis=-1, keepdims=True)

    def suffix_sum(c):
        neg_incl[c] = jnp.dot(sp16[c], neg_tri, preferred_element_type=F32)

    def weights(c):
        mask, carry_fn = chains[c][3], chains[c][4]
        w = jnp.exp2(z[c] + neg_incl[c] + carry_fn(rs))
        if mask is not None:
            w = jnp.where(mask, w, 0.0)
        a[c] = w.astype(BF16)

    def values(c):
        pv[c] = jnp.dot(a[c], chains[c][2], preferred_element_type=F32)

    stages = (scores, softplus, suffix_sum, weights, values)
    n_waves = n_c + len(stages) - 1
    done_fill = 0
    for wave in range(n_waves):
        for s in range(len(stages)):
            c = wave - s
            if 0 <= c < n_c:
                stages[s](c)
        while done_fill < (wave + 1) * len(fillers) // n_waves:
            fillers[done_fill]()
            done_fill += 1
    return [pv[c] for c in range(n_c)], [rs[c] for c in range(n_c)]


def _mixer_kernel(qkv_ref, kp_ref, vp_ref, kv_hbm, x_ref, mod_ref, bcu_ref, cuh_ref, cw_ref, w_ref, *rest,
                  gate_row, nq, n_cast):
    cast_in, o_ref, cast_out, (attn_ref, cu_ref, kbuf_ref, vbuf_ref, kv_sem) = _split_cast_refs(rest, n_cast)
    b = pl.program_id(0)
    i = pl.program_id(1)
    t = qkv_ref.shape[0]
    halo = cuh_ref.shape[0]
    width = attn_ref.shape[1]
    n_h = width // HEAD_DIM
    d = o_ref.shape[1]
    q_ref, kd_ref, vd_ref = (qkv_ref.at[:, n * width:(n + 1) * width] for n in range(3))
    gb_ref, gc_ref, u_ref = (bcu_ref.at[:, n * width:(n + 1) * width] for n in range(3))
    gch_ref, uh_ref = (cuh_ref.at[:, n * width:(n + 1) * width] for n in range(2))

    def out_proj_chunks():
        attn_prev = attn_ref[...]
        prev = gch_ref[...].astype(F32) * uh_ref[...].astype(F32)
        cu_ref[0:halo, :] = jnp.where(i == 1, 0.0, prev)
        cu_ref[halo:halo + t, :] = gc_ref[...].astype(F32) * u_ref[...].astype(F32)
        conv = cu_ref[halo:halo + t, :] * cw_ref[CONV_K - 1:CONV_K, :]
        for k in range(CONV_K - 1):
            back = CONV_K - 1 - k
            conv = conv + cu_ref[halo - back:halo - back + t, :] * cw_ref[k:k + 1, :]
        y = (gb_ref[...].astype(F32) * conv).astype(BF16)
        gate = mod_ref[0, gate_row:gate_row + 1, :]

        def out_chunk(c):
            cols = slice(c * OUT_TN, (c + 1) * OUT_TN)
            mix = jnp.dot(attn_prev, w_ref[0:width, cols], preferred_element_type=F32)
            mix = mix + jnp.dot(y, w_ref[width:, cols], preferred_element_type=F32)
            o_ref[:, cols] = x_ref[:, cols] + gate[:, cols] * mix

        return [functools.partial(out_chunk, c) for c in range(d // OUT_TN)]

    def attention(fillers, has_prev):
        row = lax.broadcasted_iota(jnp.int32, (t, t), 0)
        col = lax.broadcasted_iota(jnp.int32, (t, t), 1)
        neg_tri = jnp.where(row >= col, -1.0, 0.0).astype(BF16)
        causal = col < row

        def lanes(hd):
            return slice(hd * HEAD_DIM, (hd + 1) * HEAD_DIM)

        def heads_of(kblk_ref, vblk_ref):
            return [(kblk_ref[:, lanes(hd)], vblk_ref[:, lanes(hd)]) for hd in range(n_h)]

        def fetch_key_block(j):
            rows = pl.ds(pl.multiple_of((b * nq + j) * t, t), t)
            copies = [pltpu.make_async_copy(kv_hbm.at[rows, pl.ds(col * width, width)], buf, kv_sem.at[n])
                      for n, (col, buf) in enumerate(((K_COL, kbuf_ref), (V_COL, vbuf_ref)))]
            for cp in copies:
                cp.start()
            for cp in copies:
                cp.wait()

        qs = [q_ref[:, lanes(hd)] for hd in range(n_h)]
        zero = jnp.zeros((t, 1), F32)
        kv = heads_of(kd_ref, vd_ref)
        chains = [(qs[hd], kv[hd][0], kv[hd][1], causal, lambda rs: zero) for hd in range(n_h)]
        if has_prev:
            kv = heads_of(kp_ref, vp_ref)
            chains += [(qs[hd], kv[hd][0], kv[hd][1], None, lambda rs, hd=hd: -rs[hd]) for hd in range(n_h)]
        pvs, rss = _attn_step(chains, neg_tri, fillers)
        if has_prev:
            accs = tuple(pvs[hd] + pvs[n_h + hd] for hd in range(n_h))
            carries = tuple(-rss[hd] - rss[n_h + hd] for hd in range(n_h))

            def live_max(carries):
                live = carries[0]
                for cr in carries[1:]:
                    live = jnp.maximum(live, cr)
                return jnp.max(live)

            def cond(state):
                jj, live = state[0], state[1]
                return jnp.logical_and(jj < i, live > F32_EXP2_UNDERFLOW)

            def body(state):
                jj, _, accs, carries = state
                fetch_key_block(i - 1 - jj)
                kv = heads_of(kbuf_ref, vbuf_ref)
                pvs, rss = _attn_step([(qs[hd], kv[hd][0], kv[hd][1], None, lambda rs, hd=hd: carries[hd])
                                       for hd in range(n_h)], neg_tri)
                accs = tuple(acc + pv for acc, pv in zip(accs, pvs))
                carries = tuple(cr - rs for cr, rs in zip(carries, rss))
                return jj + 1, live_max(carries), accs, carries

            _, _, accs, _ = lax.while_loop(cond, body, (jnp.int32(1), live_max(carries), accs, carries))
        else:
            accs = pvs
        for hd in range(n_h):
            attn_ref[:, lanes(hd)] = accs[hd].astype(attn_ref.dtype)

    @pl.when(i == 0)
    def _():
        attention((), has_prev=False)
        _side_cast(cast_in, cast_out)

    @pl.when(jnp.logical_and(i > 0, i < nq))
    def _():
        attention(out_proj_chunks(), has_prev=True)
        _side_cast(cast_in, cast_out)

    @pl.when(i == nq)
    def _():
        for job in out_proj_chunks():
            job()
        _side_cast(cast_in, cast_out)


def _mixer(x2d, mod, proj, conv_w, w_out, *, gate_row, bsz, seq, n_heads, cast=(), layer=0):
    t_all, d = x2d.shape
    width = n_heads * HEAD_DIM
    t = ATT_T
    nq = seq // t
    halo = BF16_SUBLANE_TILE
    per = t // halo

    def q_row(b, i):
        return b * nq + jnp.minimum(i, nq - 1)

    def prev_row(b, i):
        return b * nq + jnp.maximum(jnp.minimum(i, nq - 1) - 1, 0)

    def out_row(b, i):
        return b * nq + jnp.maximum(i - 1, 0)

    cast_in_specs, cast_out_specs, cast_shapes, cast_vmem = _side_cast_specs(cast, layer, bsz, nq + 1)
    vmem = (2 * 5 * t * width * 2 + 2 * t * width * 2 + d * d * 2 + 2 * 2 * t * d * 4 + 2 * 4 * t * width * 2
            + (t + halo) * width * 4 + 2 * n_heads * 12 * t * t * 4 + cast_vmem + (4 << 20))
    outs = pl.pallas_call(
        functools.partial(_mixer_kernel, gate_row=gate_row, nq=nq, n_cast=len(cast)),
        grid=(bsz, nq + 1),
        in_specs=[pl.BlockSpec((t, 3 * width), lambda b, i: (q_row(b, i), 0)),
                  pl.BlockSpec((t, width), lambda b, i: (prev_row(b, i), K_COL)),
                  pl.BlockSpec((t, width), lambda b, i: (prev_row(b, i), V_COL)),
                  pl.BlockSpec(memory_space=pl.ANY),
                  pl.BlockSpec((t, d), lambda b, i: (out_row(b, i), 0)),
                  pl.BlockSpec((1, N_MOD, d), lambda b, i: (b, 0, 0)),
                  pl.BlockSpec((t, 3 * width), lambda b, i: (out_row(b, i), 1)),
                  pl.BlockSpec((halo, 2 * width), lambda b, i: (jnp.maximum(out_row(b, i) * per - 1, 0), 2)),
                  pl.BlockSpec((CONV_K, width), lambda b, i: (0, 0)),
                  pl.BlockSpec((d, d), lambda b, i: (0, 0))] + cast_in_specs,
        out_specs=[pl.BlockSpec((t, d), lambda b, i: (out_row(b, i), 0))] + cast_out_specs,
        out_shape=[jax.ShapeDtypeStruct((t_all, d), F32)] + cast_shapes,
        scratch_shapes=[pltpu.VMEM((t, width), BF16), pltpu.VMEM((t + halo, width), F32),
                        pltpu.VMEM((t, width), BF16), pltpu.VMEM((t, width), BF16),
                        pltpu.SemaphoreType.DMA((2,))],
        compiler_params=_params(("arbitrary", "arbitrary"), vmem),
        name="mixer",
    )(proj, proj, proj, proj, x2d, mod, proj, proj, conv_w, w_out, *cast)
    return outs[0], outs[1:]


def kernel(x, c, w_ada, b_ada, w1_gu, w1_down, w_in, q_norm_w, k_norm_w, conv_w, w_out, w2_gu, w2_down):
    bsz, seq, d = x.shape
    depth = w_ada.shape[0]
    width = conv_w.shape[2]
    n_heads = width // HEAD_DIM
    assert w_in.shape[2] == 6 * width and width == PROJ_TN
    assert seq % max(FFN_TM, PROJ_TM, ATT_T) == 0 and d % OUT_TN == 0

    x2d = x.reshape(bsz * seq, d)
    for l in range(depth):
        mod = _adaln(c, w_ada[l], b_ada[l]).reshape(bsz, N_MOD, d)
        head, w1_g_b, w1_u_b, w1_down_b = _ffn_head(x2d, mod, w1_gu, w1_down, shift_row=0, layer=l)
        x2d, (w_in_b, w_out_b) = _ffn(
            x2d, mod, (w1_g_b, 0), (w1_u_b, 0), w1_down_b, shift_row=0, seq=seq, head=head,
            cast=(w_in, w_out), layer=l)
        proj, (w2_gu_b,) = _in_proj(
            x2d, mod, w_in_b, q_norm_w[l].reshape(1, HEAD_DIM), k_norm_w[l].reshape(1, HEAD_DIM),
            shift_row=3, seq=seq, cast=(w2_gu,), layer=l)
        x2d, (w2_down_b,) = _mixer(x2d, mod, proj, conv_w[l], w_out_b, gate_row=5, bsz=bsz, seq=seq,
                                   n_heads=n_heads, cast=(w2_down,), layer=l)
        n_f = w2_down_b.shape[0] // FFN_TF
        x2d, _ = _ffn(x2d, mod, (w2_gu_b, 0), (w2_gu_b, n_f), w2_down_b, shift_row=6, seq=seq)
    return x2d.reshape(bsz, seq, d)
```
